```python
import math
import jax
import jax.numpy as jnp
from jax import lax
import numpy as np

D_MODEL = 1024
BATCH = 2
SEQ = 16384
DEPTH = 2

GRID_W = 64
CTX_LEN = 256
N_AB_LAYERS = (DEPTH + 1) // 2
N_SSD_LAYERS = DEPTH // 2

HGRN_HEADS = 4
HGRN_DK = 128
HGRN_DV = 128
HGRN_WIDTH = HGRN_HEADS * HGRN_DK
GLA_CHUNK = 32

NA_HEADS = 8
NA_DH = 64
NA_WIDTH = NA_HEADS * NA_DH
NA_KR = 8
NA_KC = 16
NA_QCB = 16
NA_NCB = GRID_W // NA_QCB
NA_BAND_W = NA_QCB + NA_KC

AB_IN = 5 * HGRN_WIDTH + 3 * NA_WIDTH
AB_MIX = HGRN_HEADS * HGRN_DV + NA_WIDTH

SSD_INNER = 2 * D_MODEL
SSD_HEADDIM = 64
SSD_HEADS = SSD_INNER // SSD_HEADDIM
SSD_GROUPS = 8
SSD_HPG = SSD_HEADS // SSD_GROUPS
SSD_STATE = 128
SSD_CONV = 5
SSD_CHUNK = 64
SSD_CONV_DIM = SSD_INNER + 2 * SSD_GROUPS * SSD_STATE
SSD_IN = SSD_INNER + SSD_CONV_DIM + 2 * SSD_HEADS

N_EXPERTS = 16
EXPERT_FF = 1024
CAPACITY_FACTOR = 2

RMS_EPS = 1e-6
NEG_INF = -1e30
F32 = jnp.float32

kernel_name = 'hybrid_hgrn2_natten_ssd_ecmoe_dit'


def rmsnorm(x, g):
    xf = x.astype(F32)
    y = xf * lax.rsqrt(jnp.mean(xf * xf, axis=-1, keepdims=True) + RMS_EPS)
    return (y * g.astype(F32)).astype(x.dtype)


def modulate(h, shift, scale):
    return h * (1 + scale) + shift


def hgrn_heads(a):
    b, t, _ = a.shape
    return a.reshape(b, t, HGRN_HEADS, -1).transpose(0, 2, 1, 3)


def hgrn_gates(z, lb):
    f = lb + (1 - lb) * jax.nn.sigmoid(z)
    return hgrn_heads(jnp.log(f)), hgrn_heads(1 - f)


def gla_chunked(q, k, v, logf, s0):
    b, h, t, _ = q.shape
    n = t // GLA_CHUNK

    def chunks(a):
        return jnp.moveaxis(a.reshape(b, h, n, GLA_CHUNK, a.shape[-1]), 2, 0)

    qc, kc, vc, gc = chunks(q), chunks(k), chunks(v), chunks(logf)
    cum = jnp.cumsum(gc, axis=-2)
    cum_last = cum[..., -1:, :]
    q_in = qc * jnp.exp(cum)
    k_in = kc * jnp.exp(-cum)
    k_out = kc * jnp.exp(cum_last - cum)
    tril = np.tril(np.ones((GLA_CHUNK, GLA_CHUNK), bool))
    attn = jnp.where(tril, jnp.einsum('nbhld,nbhsd->nbhls', q_in, k_in), 0.0)
    intra = jnp.einsum('nbhls,nbhsv->nbhlv', attn, vc)

    def step(s, xs):
        q_i, ko_i, v_i, cl_i, intra_i = xs
        o = intra_i + jnp.einsum('bhld,bhdv->bhlv', q_i, s)
        s = jnp.exp(cl_i)[..., 0, :, None] * s + jnp.einsum('bhld,bhlv->bhdv', ko_i, v_i)
        return s, o

    s_fin, o = lax.scan(step, s0, (q_in, k_out, vc, cum_last, intra))
    return jnp.moveaxis(o, 0, 2).reshape(b, h, t, v.shape[-1]), s_fin


def hgrn_out_gate(o, g, gain):
    b, h, t, dv = o.shape
    o = o.transpose(0, 2, 1, 3)
    o = o * lax.rsqrt(jnp.mean(o * o, axis=-1, keepdims=True) + RMS_EPS) * gain.astype(F32)
    return o.reshape(b, t, h * dv) * jax.nn.silu(g)


def hgrn2_mixer(p_lat, p_ctx, lb, onorm_g, need_ctx):
    def split(p):
        q, zf, zb, i, g = jnp.split(p.astype(F32), 5, axis=-1)
        return hgrn_heads(jax.nn.silu(q)), zf, zb, hgrn_heads(i), g

    ql, zfl, zbl, il, gl = split(p_lat)
    qc, zfc, zbc, ic, gc = split(p_ctx)
    s0 = jnp.zeros((p_lat.shape[0], HGRN_HEADS, HGRN_DK, HGRN_DV), F32)

    def direction(zl, zc, lb_d, flip):
        lfl, kl = hgrn_gates(zl, lb_d)
        lfc, kc = hgrn_gates(zc, lb_d)
        lat = (ql, kl, il, lfl)
        ctx = (qc, kc, ic, lfc)
        if flip:
            lat = tuple(jnp.flip(a, axis=2) for a in lat)
            ctx = tuple(jnp.flip(a, axis=2) for a in ctx)
        oc, sc = gla_chunked(*ctx, s0)
        ol, _ = gla_chunked(*lat, sc)
        if flip:
            ol, oc = jnp.flip(ol, axis=2), jnp.flip(oc, axis=2)
        return ol, oc

    olf, ocf = direction(zfl, zfc, lb[0], False)
    olb, ocb = direction(zbl, zbc, lb[1], True)
    out_l = hgrn_out_gate(olf + olb, gl, onorm_g).astype(p_lat.dtype)
    out_c = hgrn_out_gate(ocf + ocb, gc, onorm_g).astype(p_ctx.dtype) if need_ctx else None
    return out_l, out_c


def na_heads(a):
    b, t, _ = a.shape
    return a.reshape(b, t, NA_HEADS, NA_DH).transpose(0, 2, 1, 3)


def na_merge(o):
    b, h, t, d = o.shape
    return o.transpose(0, 2, 1, 3).reshape(b, t, h * d)


def neighborhood_attention(q, k, v, kc, vc, rpb):
    b, h, s, dh = q.shape
    rows = s // GRID_W
    kr = min(NA_KR, rows)
    qcol = np.arange(GRID_W).reshape(NA_NCB, NA_QCB)
    win0 = np.clip(qcol - NA_KC // 2, 0, GRID_W - NA_KC)
    band0 = np.minimum(win0[:, 0], GRID_W - NA_BAND_W)
    band_cols = band0[:, None] + np.arange(NA_BAND_W)
    col_ok = (band_cols[:, None, :] >= win0[:, :, None]) & (band_cols[:, None, :] < win0[:, :, None] + NA_KC)
    dc_idx = np.clip(band_cols[:, None, :] - qcol[:, :, None] + NA_KC - 1, 0, 2 * NA_KC - 2)
    rpb_c = rpb[:, :, dc_idx].astype(F32)
    col_mask = col_ok[:, :, None, :]
    kg = k.reshape(b, h, rows, GRID_W, dh)
    vg = v.reshape(b, h, rows, GRID_W, dh)
    q_rows = jnp.moveaxis(q.reshape(b, h, rows, NA_NCB, NA_QCB, dh), 2, 0)
    n_win = kr * NA_BAND_W

    def row_block(args):
        r, qr = args
        r0 = jnp.clip(r - kr // 2, 0, rows - kr)
        k_band = lax.dynamic_slice_in_dim(kg, r0, kr, axis=2)[:, :, :, band_cols]
        v_band = lax.dynamic_slice_in_dim(vg, r0, kr, axis=2)[:, :, :, band_cols]
        s_win = jnp.einsum('bhjqd,bhkjcd->bhjqkc', qr, k_band).astype(F32)
        dr_idx = r0 + jnp.arange(kr) - r + NA_KR - 1
        bias = jnp.transpose(rpb_c[:, dr_idx], (0, 2, 3, 1, 4))
        s_win = jnp.where(col_mask, s_win + bias, NEG_INF)
        s_ctx = jnp.einsum('bhjqd,bhmd->bhjqm', qr, kc).astype(F32)
        logits = jnp.concatenate([s_win.reshape(b, h, NA_NCB, NA_QCB, n_win), s_ctx], axis=-1)
        p = jax.nn.softmax(logits, axis=-1).astype(v.dtype)
        p_win = p[..., :n_win].reshape(b, h, NA_NCB, NA_QCB, kr, NA_BAND_W)
        return (jnp.einsum('bhjqkc,bhkjcd->bhjqd', p_win, v_band)
                + jnp.einsum('bhjqm,bhmd->bhjqd', p[..., n_win:], vc))

    o = lax.map(row_block, (jnp.arange(rows), q_rows))
    return jnp.moveaxis(o, 0, 2).reshape(b, h, s, dh)


def context_attention(q, k, v):
    p = jax.nn.softmax(jnp.einsum('bhqd,bhkd->bhqk', q, k).astype(F32), axis=-1).astype(v.dtype)
    return jnp.einsum('bhqk,bhkd->bhqd', p, v)


def ab_mixer(h_lat, h_ctx, w_in, w_out, lb, onorm_g, rpb, need_ctx):
    p_lat = h_lat @ w_in
    p_ctx = h_ctx @ w_in
    hg = 5 * HGRN_WIDTH
    hl, hc = hgrn2_mixer(p_lat[..., :hg], p_ctx[..., :hg], lb, onorm_g, need_ctx)
    ql, kl, vl = (na_heads(a) for a in jnp.split(p_lat[..., hg:], 3, axis=-1))
    qc, kc, vc = (na_heads(a) for a in jnp.split(p_ctx[..., hg:], 3, axis=-1))
    scale = NA_DH ** -0.5
    nl = neighborhood_attention(ql * scale, kl, vl, kc, vc, rpb)
    y_lat = jnp.concatenate([hl, na_merge(nl)], axis=-1) @ w_out
    y_ctx = None
    if need_ctx:
        nc = context_attention(qc * scale, kc, vc)
        y_ctx = jnp.concatenate([hc, na_merge(nc)], axis=-1) @ w_out
    return y_lat, y_ctx


def dwconv_centred(u, w, bias):
    ch = u.shape[-1]
    y = lax.conv_general_dilated(u, w[:, None, :].astype(u.dtype), window_strides=(1,),
                                 padding=[(SSD_CONV // 2, SSD_CONV // 2)],
                                 dimension_numbers=('NWC', 'WIO', 'NWC'), feature_group_count=ch)
    return y + bias


def ssd_chunked(x, dt, a, bm, cm, h0):
    b, t = x.shape[:2]
    n = t // SSD_CHUNK

    def chunks(arr):
        return jnp.moveaxis(arr.reshape((b, n, SSD_CHUNK) + arr.shape[2:]), 1, 0)

    xc, dtc, bc, cc = chunks(x), chunks(dt), chunks(bm), chunks(cm)
    acum = jnp.cumsum(dtc * a, axis=2)
    causal = np.tril(np.ones((SSD_CHUNK, SSD_CHUNK), bool))[:, :, None, None]
    seg = acum[:, :, :, None] - acum[:, :, None, :]
    decay = jnp.exp(jnp.where(causal, seg, -jnp.inf))
    cb = jnp.einsum('nblgk,nbsgk->nblsg', cc, bc)
    y_diag = jnp.einsum('nblsgh,nbsghp->nblghp', cb[..., None] * decay * dtc[:, :, None], xc)
    w_end = jnp.exp(acum[:, :, -1:] - acum) * dtc

    def step(hs, xs):
        y_d, x_i, b_i, c_i, w_i, a_i = xs
        y = y_d + jnp.einsum('blgk,bghpk->blghp', c_i, hs) * jnp.exp(a_i)[..., None]
        hs = (jnp.exp(a_i[:, -1])[..., None, None] * hs
              + jnp.einsum('blgk,blghp->bghpk', b_i, x_i * w_i[..., None]))
        return hs, y

    h_fin, y = lax.scan(step, h0, (y_diag, xc, bc, cc, w_end, acum))
    return jnp.moveaxis(y, 0, 1).reshape(x.shape), h_fin


def ssd_mixer(h_lat, h_ctx, w_in, conv_w, conv_b, a_log, dt_bias, d_skip, norm_g, w_out, need_ctx):
    gn = SSD_GROUPS * SSD_STATE

    def prep(h):
        bsz, t, _ = h.shape
        p = h @ w_in
        z = p[..., :SSD_INNER]
        xbc = jax.nn.silu(dwconv_centred(p[..., SSD_INNER:SSD_INNER + SSD_CONV_DIM], conv_w, conv_b)).astype(F32)
        xs = xbc[..., :SSD_INNER].reshape(bsz, t, SSD_GROUPS, SSD_HPG, SSD_HEADDIM)
        bm = xbc[..., SSD_INNER:SSD_INNER + gn].reshape(bsz, t, SSD_GROUPS, SSD_STATE)
        cm = xbc[..., SSD_INNER + gn:].reshape(bsz, t, SSD_GROUPS, SSD_STATE)
        dt_raw = p[..., SSD_INNER + SSD_CONV_DIM:].astype(F32).reshape(bsz, t, 2, SSD_GROUPS, SSD_HPG)
        dt = jax.nn.softplus(dt_raw + dt_bias.astype(F32).reshape(2, SSD_GROUPS, SSD_HPG))
        return z, xs, bm, cm, dt

    zl, xl, bl, cl, dtl = prep(h_lat)
    zc, xc, bc, cc, dtc = prep(h_ctx)
    a = -jnp.exp(a_log.astype(F32)).reshape(2, SSD_GROUPS, SSD_HPG)
    h0 = jnp.zeros((h_lat.shape[0], SSD_GROUPS, SSD_HPG, SSD_HEADDIM, SSD_STATE), F32)

    def direction(d, flip):
        lat = (xl, dtl[:, :, d], bl, cl)
        ctx = (xc, dtc[:, :, d], bc, cc)
        if flip:
            lat = tuple(jnp.flip(arr, axis=1) for arr in lat)
            ctx = tuple(jnp.flip(arr, axis=1) for arr in ctx)
        yc, hc = ssd_chunked(ctx[0], ctx[1], a[d], ctx[2], ctx[3], h0)
        yl, _ = ssd_chunked(lat[0], lat[1], a[d], lat[2], lat[3], hc)
        if flip:
            yl, yc = jnp.flip(yl, axis=1), jnp.flip(yc, axis=1)
        return yl, yc

    ylf, ycf = direction(0, False)
    ylb, ycb = direction(1, True)
    dsk = d_skip.astype(F32).reshape(SSD_GROUPS, SSD_HPG, 1)

    def finish(yf, yb, xs, z, h):
        bsz, t = xs.shape[:2]
        y = (yf + yb + dsk * xs).reshape(bsz, t, SSD_INNER) * jax.nn.silu(z.astype(F32))
        yg = y.reshape(bsz, t, SSD_GROUPS, SSD_INNER // SSD_GROUPS)
        yg = yg * lax.rsqrt(jnp.mean(yg * yg, axis=-1, keepdims=True) + RMS_EPS)
        y = yg.reshape(bsz, t, SSD_INNER) * norm_g.astype(F32)
        return y.astype(h.dtype) @ w_out

    y_lat = finish(ylf, ylb, xl, zl, h_lat)
    y_ctx = finish(ycf, ycb, xc, zc, h_ctx) if need_ctx else None
    return y_lat, y_ctx


def expert_choice_ffn(h, w_router, w1, w3, w2):
    bsz, n, d = h.shape
    cap = max(1, CAPACITY_FACTOR * n // N_EXPERTS)
    aff = jax.nn.softmax(jnp.einsum('bnd,de->bne', h, w_router).astype(F32), axis=-1)
    gate, idx = lax.top_k(jnp.swapaxes(aff, 1, 2), cap)
    xs = jax.vmap(lambda hb, ib: hb[ib])(h, idx)
    hid = jax.nn.silu(jnp.einsum('becd,edf->becf', xs, w1)) * jnp.einsum('becd,edf->becf', xs, w3)
    ye = (jnp.einsum('becf,efd->becd', hid, w2) * gate[..., None]).astype(h.dtype)

    def combine(yb, ib):
        return jnp.zeros((n, d), h.dtype).at[ib.reshape(-1)].add(yb.reshape(-1, d))

    return jax.vmap(combine)(ye, idx)


def setup_inputs(seed: int = 0) -> dict:
    key = jax.random.key(seed)
    ks = iter(jax.random.split(key, 32))
    D = D_MODEL

    def nrm(shape, s):
        return jax.random.normal(next(ks), shape, F32) * s

    u_dt = jax.random.uniform(next(ks), (N_SSD_LAYERS, 2, SSD_HEADS), F32)
    dt0 = jnp.exp(u_dt * (math.log(0.1) - math.log(1e-3)) + math.log(1e-3))
    a_init = jax.random.uniform(next(ks), (N_SSD_LAYERS, 2, SSD_HEADS), F32, 1.0, 16.0)
    return {
        'x': nrm((BATCH, SEQ, D), 1.0),
        'c': nrm((BATCH, D), 1.0),
        'ctx': nrm((BATCH, CTX_LEN, D), 1.0),
        'c_ctx': nrm((D,), 1.0),
        'ada_w': nrm((DEPTH, D, 6 * D), 0.5 * D ** -0.5),
        'ada_b': nrm((DEPTH, 6 * D), 0.02),
        'norm_g': 1.0 + nrm((DEPTH, 2, D), 0.02),
        'final_g': 1.0 + nrm((D,), 0.02),
        'ab_w_in': nrm((N_AB_LAYERS, D, AB_IN), D ** -0.5),
        'ab_w_out': nrm((N_AB_LAYERS, AB_MIX, D), AB_MIX ** -0.5),
        'hgrn_lb_logits': nrm((2, N_AB_LAYERS + 1, HGRN_WIDTH), 0.1),
        'hgrn_onorm_g': 1.0 + nrm((N_AB_LAYERS, HGRN_DV), 0.02),
        'na_rpb': nrm((N_AB_LAYERS, NA_HEADS, 2 * NA_KR - 1, 2 * NA_KC - 1), 0.1),
        'ssd_w_in': nrm((N_SSD_LAYERS, D, SSD_IN), D ** -0.5),
        'ssd_conv_w': nrm((N_SSD_LAYERS, SSD_CONV, SSD_CONV_DIM), SSD_CONV ** -0.5),
        'ssd_conv_b': nrm((N_SSD_LAYERS, SSD_CONV_DIM), 0.02),
        'ssd_a_log': jnp.log(a_init),
        'ssd_dt_bias': dt0 + jnp.log(-jnp.expm1(-dt0)),
        'ssd_d': 1.0 + nrm((N_SSD_LAYERS, SSD_HEADS), 0.1),
        'ssd_norm_g': 1.0 + nrm((N_SSD_LAYERS, SSD_INNER), 0.02),
        'ssd_w_out': nrm((N_SSD_LAYERS, SSD_INNER, D), SSD_INNER ** -0.5),
        'moe_router': nrm((DEPTH, D, N_EXPERTS), D ** -0.5),
        'moe_w1': nrm((DEPTH, N_EXPERTS, D, EXPERT_FF), D ** -0.5),
        'moe_w3': nrm((DEPTH, N_EXPERTS, D, EXPERT_FF), D ** -0.5),
        'moe_w2': nrm((DEPTH, N_EXPERTS, EXPERT_FF, D), EXPERT_FF ** -0.5),
    }


def reference(x, c, ctx, c_ctx, ada_w, ada_b, norm_g, final_g, ab_w_in, ab_w_out, hgrn_lb_logits,
              hgrn_onorm_g, na_rpb, ssd_w_in, ssd_conv_w, ssd_conv_b, ssd_a_log, ssd_dt_bias, ssd_d,
              ssd_norm_g, ssd_w_out, moe_router, moe_w1, moe_w3, moe_w2):
    s_lat = jax.nn.silu(c)
    s_ctx = jax.nn.silu(c_ctx)
    lb_all = jnp.cumsum(jax.nn.softmax(hgrn_lb_logits.astype(F32), axis=1), axis=1)
    for l in range(DEPTH):
        need_ctx = l < DEPTH - 1
        mod_l = jnp.split((s_lat @ ada_w[l] + ada_b[l])[:, None, :], 6, axis=-1)
        mod_c = jnp.split(s_ctx @ ada_w[l] + ada_b[l], 6, axis=-1)
        h_lat = modulate(rmsnorm(x, norm_g[l, 0]), mod_l[0], mod_l[1])
        h_ctx = modulate(rmsnorm(ctx, norm_g[l, 0]), mod_c[0], mod_c[1])
        k = l // 2
        if l % 2 == 0:
            y_lat, y_ctx = ab_mixer(h_lat, h_ctx, ab_w_in[k], ab_w_out[k], lb_all[:, k],
                                    hgrn_onorm_g[k], na_rpb[k], need_ctx)
        else:
            y_lat, y_ctx = ssd_mixer(h_lat, h_ctx, ssd_w_in[k], ssd_conv_w[k], ssd_conv_b[k], ssd_a_log[k],
                                     ssd_dt_bias[k], ssd_d[k], ssd_norm_g[k], ssd_w_out[k], need_ctx)
        x = x + mod_l[2] * y_lat
        f_lat = modulate(rmsnorm(x, norm_g[l, 1]), mod_l[3], mod_l[4])
        x = x + mod_l[5] * expert_choice_ffn(f_lat, moe_router[l], moe_w1[l], moe_w3[l], moe_w2[l])
        if need_ctx:
            ctx = ctx + mod_c[2] * y_ctx
            f_ctx = modulate(rmsnorm(ctx, norm_g[l, 1]), mod_c[3], mod_c[4])
            ctx = ctx + mod_c[5] * expert_choice_ffn(f_ctx, moe_router[l], moe_w1[l], moe_w3[l], moe_w2[l])
    return rmsnorm(x, final_g)
```

```python
import functools
import math

import jax
import jax.numpy as jnp
import numpy as np
from jax import lax
from jax.experimental import pallas as pl
from jax.experimental.pallas import tpu as pltpu

F32 = jnp.float32
BF16 = jnp.bfloat16
I32 = jnp.int32

GRID_W = 64
HGRN_HEADS = 4
HGRN_DK = 128
HGRN_WIDTH = HGRN_HEADS * HGRN_DK
GLA_CHUNK = 32
NA_HEADS = 8
NA_DH = 64
NA_WIDTH = NA_HEADS * NA_DH
NA_KR = 8
NA_KC = 16
SSD_HEADDIM = 64
SSD_GROUPS = 8
SSD_STATE = 128
SSD_CONV = 5
SSD_CHUNK = 64
N_EXPERTS = 16
CAPACITY_FACTOR = 2
RMS_EPS = 1e-6
NEG_INF = -1e30

LANES = 128
SUBLANES = 8
VMEM_LIMIT_BYTES = 56 * 1024 * 1024


def _cparams(*sem):
    return pltpu.CompilerParams(dimension_semantics=sem, vmem_limit_bytes=VMEM_LIMIT_BYTES)


def _sigmoid(x):
    return 1.0 / (1.0 + jnp.exp(-x))


def _silu(x):
    return x * _sigmoid(x)


def _split3(x):
    a = x.astype(BF16)
    r = x - a.astype(F32)
    b = r.astype(BF16)
    c = (r - b.astype(F32)).astype(BF16)
    return a, b, c


def _dot(a, b):
    return jnp.dot(a, b, preferred_element_type=F32)


def _dot_nt(a, b):
    return lax.dot_general(a, b, (((1,), (1,)), ((), ())), preferred_element_type=F32)


def _dot_tn(a, b):
    return lax.dot_general(a, b, (((0,), (0,)), ((), ())), preferred_element_type=F32)


def _dot_exact_lhs(m01, x):
    m = m01.astype(BF16)
    a, b, c = _split3(x)
    return _dot(m, a) + _dot(m, b) + _dot(m, c)


def _dot3(a, b):
    a1 = a.astype(BF16)
    a2 = (a - a1.astype(F32)).astype(BF16)
    b1 = b.astype(BF16)
    b2 = (b - b1.astype(F32)).astype(BF16)
    return _dot(a1, b1) + _dot(a1, b2) + _dot(a2, b1)


def _dot3_nt(a, b):
    a1 = a.astype(BF16)
    a2 = (a - a1.astype(F32)).astype(BF16)
    b1 = b.astype(BF16)
    b2 = (b - b1.astype(F32)).astype(BF16)
    return _dot_nt(a1, b1) + _dot_nt(a1, b2) + _dot_nt(a2, b1)


def _rms_mod(x, g, shift, scale):
    y = x * lax.rsqrt(jnp.mean(x * x, axis=-1, keepdims=True) + RMS_EPS) * g
    return y * (1.0 + scale) + shift


def _mod_kernel(s_ref, w_ref, b_ref, o_ref):
    s = _silu(s_ref[...])
    o_ref[0] = _dot3(s, w_ref[0]) + b_ref[0]


def modulation_table(c, c_ctx, ada_w, ada_b):
    depth, d, n = ada_w.shape
    bsz = c.shape[0]
    s = jnp.zeros((SUBLANES, d), F32).at[:bsz].set(c).at[bsz].set(c_ctx)
    tn = 1024
    return pl.pallas_call(
        _mod_kernel,
        grid=(depth, n // tn),
        in_specs=[pl.BlockSpec((SUBLANES, d), lambda l, j: (0, 0)),
                  pl.BlockSpec((1, d, tn), lambda l, j: (l, 0, j)),
                  pl.BlockSpec((1, 1, tn), lambda l, j: (l, 0, j))],
        out_specs=pl.BlockSpec((1, SUBLANES, tn), lambda l, j: (l, 0, j)),
        out_shape=jax.ShapeDtypeStruct((depth, SUBLANES, n), F32),
        compiler_params=_cparams("parallel", "parallel"),
        name="mod_table",
    )(s, ada_w, ada_b.reshape(depth, 1, n))


def _proj_kernel(x_ref, g_ref, sh_ref, sc_ref, w_ref, o_ref, h_scr):
    @pl.when(pl.program_id(1) == 0)
    def _():
        h_scr[...] = _rms_mod(x_ref[...], g_ref[...], sh_ref[0], sc_ref[0]).astype(BF16)

    o_ref[...] = _dot(h_scr[...], w_ref[...])


def norm_mod_project(x, g, shift, scale, w, *, rows_per_mod, tm, tn):
    r, d = x.shape
    n = w.shape[1]
    tm = min(tm, rows_per_mod)
    per = rows_per_mod // tm
    return pl.pallas_call(
        _proj_kernel,
        grid=(r // tm, n // tn),
        in_specs=[pl.BlockSpec((tm, d), lambda i, j: (i, 0)),
                  pl.BlockSpec((1, d), lambda i, j: (0, 0)),
                  pl.BlockSpec((1, 1, d), lambda i, j: (i // per, 0, 0)),
                  pl.BlockSpec((1, 1, d), lambda i, j: (i // per, 0, 0)),
                  pl.BlockSpec((d, tn), lambda i, j: (0, j))],
        out_specs=pl.BlockSpec((tm, tn), lambda i, j: (i, j)),
        out_shape=jax.ShapeDtypeStruct((r, n), F32),
        scratch_shapes=[pltpu.VMEM((tm, d), BF16)],
        compiler_params=_cparams("parallel", "arbitrary"),
        name="norm_mod_project",
    )(x, g.reshape(1, d), shift, scale, w)


def _tri(n, upper):
    r = lax.broadcasted_iota(I32, (n, n), 0)
    c = lax.broadcasted_iota(I32, (n, n), 1)
    return (r <= c) if upper else (r >= c)


def _hgrn_chunk(q, z, v, lb, st, upper):
    n = q.shape[0]
    mask = _tri(n, upper)
    qs = _silu(q)
    f = lb + (1.0 - lb) * _sigmoid(z)
    g = jnp.log(f)
    k = 1.0 - f
    cum = _dot_exact_lhs(mask.astype(F32), g)
    cl = cum[0:1] if upper else cum[n - 1:n]
    q_in = (qs * jnp.exp(cum)).astype(BF16)
    k_in = (k * jnp.exp(-cum)).astype(BF16)
    k_out = (k * jnp.exp(cl - cum)).astype(BF16)
    vb = v.astype(BF16)
    attn = jnp.where(mask, _dot_nt(q_in, k_in), 0.0)
    o = _dot(attn.astype(BF16), vb) + _dot_nt(q_in, st.astype(BF16))
    st = st * jnp.exp(cl) + _dot_tn(vb, k_out)
    return o, st


def _hgrn_kernel(qf_ref, zf_ref, vf_ref, qb_ref, zb_ref, vb_ref,
                 qc_ref, zfc_ref, zbc_ref, vc_ref, lb_ref,
                 of_ref, ob_ref, ocf_ref, ocb_ref, stf_ref, stb_ref, *, tb, lc):
    ch = GLA_CHUNK
    lbf = lb_ref[0:1]
    lbb = lb_ref[1:2]

    @pl.when(pl.program_id(1) == 0)
    def _():
        ncc = lc // ch

        def cbody(c, carry):
            sf, sb = carry
            rf = pl.ds(pl.multiple_of(c * ch, ch), ch)
            rb = pl.ds(pl.multiple_of((ncc - 1 - c) * ch, ch), ch)
            o, sf = _hgrn_chunk(qc_ref[rf, :], zfc_ref[rf, :], vc_ref[rf, :], lbf, sf, False)
            ocf_ref[rf, :] = o
            o, sb = _hgrn_chunk(qc_ref[rb, :], zbc_ref[rb, :], vc_ref[rb, :], lbb, sb, True)
            ocb_ref[rb, :] = o
            return sf, sb

        z0 = jnp.zeros((HGRN_DK, HGRN_DK), F32)
        sf, sb = lax.fori_loop(0, ncc, cbody, (z0, z0))
        stf_ref[...] = sf
        stb_ref[...] = sb

    nc = tb // ch

    def body(c, carry):
        sf, sb = carry
        rf = pl.ds(pl.multiple_of(c * ch, ch), ch)
        rb = pl.ds(pl.multiple_of((nc - 1 - c) * ch, ch), ch)
        o, sf = _hgrn_chunk(qf_ref[rf, :], zf_ref[rf, :], vf_ref[rf, :], lbf, sf, False)
        of_ref[rf, :] = o
        o, sb = _hgrn_chunk(qb_ref[rb, :], zb_ref[rb, :], vb_ref[rb, :], lbb, sb, True)
        ob_ref[rb, :] = o
        return sf, sb

    sf, sb = lax.fori_loop(0, nc, body, (stf_ref[...], stb_ref[...]))
    stf_ref[...] = sf
    stb_ref[...] = sb


def hgrn_scan(p_lat, p_ctx, lb, *, bsz, seq, lc, tb):
    h = HGRN_HEADS
    dk = HGRN_DK
    nb = seq // tb
    cq, czf, czb, cv = 0, h, 2 * h, 3 * h

    def lat(col, rev):
        if rev:
            return pl.BlockSpec((tb, dk), lambda bh, i: ((bh // h) * nb + nb - 1 - i, col + bh % h))
        return pl.BlockSpec((tb, dk), lambda bh, i: ((bh // h) * nb + i, col + bh % h))

    def ctx(col):
        return pl.BlockSpec((lc, dk), lambda bh, i: (bh // h, col + bh % h))

    def out_lat(rev):
        if rev:
            return pl.BlockSpec((tb, dk), lambda bh, i: ((bh // h) * nb + nb - 1 - i, bh % h))
        return pl.BlockSpec((tb, dk), lambda bh, i: ((bh // h) * nb + i, bh % h))

    out_ctx = pl.BlockSpec((lc, dk), lambda bh, i: (bh // h, bh % h))
    w = h * dk
    return pl.pallas_call(
        functools.partial(_hgrn_kernel, tb=tb, lc=lc),
        grid=(bsz * h, nb),
        in_specs=[lat(cq, False), lat(czf, False), lat(cv, False),
                  lat(cq, True), lat(czb, True), lat(cv, True),
                  ctx(cq), ctx(czf), ctx(czb), ctx(cv),
                  pl.BlockSpec((2, dk), lambda bh, i: (0, bh % h))],
        out_specs=[out_lat(False), out_lat(True), out_ctx, out_ctx],
        out_shape=[jax.ShapeDtypeStruct((bsz * seq, w), F32), jax.ShapeDtypeStruct((bsz * seq, w), F32),
                   jax.ShapeDtypeStruct((bsz * lc, w), F32), jax.ShapeDtypeStruct((bsz * lc, w), F32)],
        scratch_shapes=[pltpu.VMEM((dk, dk), F32), pltpu.VMEM((dk, dk), F32)],
        compiler_params=_cparams("parallel", "arbitrary"),
        name="hgrn_scan",
    )(p_lat, p_lat, p_lat, p_lat, p_lat, p_lat, p_ctx, p_ctx, p_ctx, p_ctx, lb)


NA_RB = 8


def na_bias_table(rpb):
    w = GRID_W
    qc = np.arange(w)
    win0 = np.clip(qc - NA_KC // 2, 0, w - NA_KC)
    kc = np.arange(w)
    ok = (kc[None, :] >= win0[:, None]) & (kc[None, :] < win0[:, None] + NA_KC)
    dc = np.clip(kc[None, :] - qc[:, None] + NA_KC - 1, 0, 2 * NA_KC - 2)
    v = np.arange(NA_KR)
    j = np.arange(NA_KR)
    dr = j[None, :] - v[:, None] + NA_KR - 1
    t = rpb.astype(F32)[:, dr][:, :, :, dc]
    t = jnp.where(ok[None, None, None], t, NEG_INF)
    t = jnp.transpose(t, (0, 1, 3, 2, 4))
    return t.reshape(rpb.shape[0], NA_KR, w, NA_KR * w)


def _na_kernel(q_ref, kp_ref, kc_ref, kn_ref, vp_ref, vc_ref, vn_ref, kx_ref, vx_ref, bias_ref,
               o_ref, ks_ref, vs_ref, *, rows):
    w = GRID_W
    blk = NA_RB * w
    i = pl.program_id(2)
    ks_ref[0:blk, :] = kp_ref[...].astype(BF16)
    ks_ref[blk:2 * blk, :] = kc_ref[...].astype(BF16)
    ks_ref[2 * blk:3 * blk, :] = kn_ref[...].astype(BF16)
    vs_ref[0:blk, :] = vp_ref[...].astype(BF16)
    vs_ref[blk:2 * blk, :] = vc_ref[...].astype(BF16)
    vs_ref[2 * blk:3 * blk, :] = vn_ref[...].astype(BF16)
    kx = kx_ref[...].astype(BF16)
    vx = vx_ref[...].astype(BF16)
    scale = NA_DH ** -0.5
    for rr in range(NA_RB):
        r = i * NA_RB + rr
        r0 = jnp.clip(r - NA_KR // 2, 0, rows - NA_KR)
        start = pl.multiple_of((r0 - (i - 1) * NA_RB) * w, w)
        vv = r - r0
        outs = []
        for hh in range(2):
            cs = slice(hh * NA_DH, (hh + 1) * NA_DH)
            q = (q_ref[rr * w:(rr + 1) * w, cs] * scale).astype(BF16)
            kw = ks_ref[pl.ds(start, NA_KR * w), cs]
            vw = vs_ref[pl.ds(start, NA_KR * w), cs]
            s = _dot_nt(q, kw) + bias_ref[hh, vv]
            sx = _dot_nt(q, kx[:, cs])
            m = jnp.maximum(jnp.max(s, axis=-1, keepdims=True), jnp.max(sx, axis=-1, keepdims=True))
            p = jnp.exp(s - m)
            px = jnp.exp(sx - m)
            den = jnp.sum(p, axis=-1, keepdims=True) + jnp.sum(px, axis=-1, keepdims=True)
            o = _dot(p.astype(BF16), vw) + _dot(px.astype(BF16), vx[:, cs])
            outs.append(o / den)
        o_ref[rr * w:(rr + 1) * w, :] = jnp.concatenate(outs, axis=-1).astype(o_ref.dtype)


def neighborhood_attention(p_lat, p_ctx, bias_tbl, *, bsz, seq, lc, col0):
    w = GRID_W
    rows = seq // w
    blk = NA_RB * w
    nb = rows // NA_RB
    hp = NA_HEADS // 2
    pw = 2 * NA_DH
    cq = col0 // pw
    ck = cq + hp
    cv = ck + hp

    def lat(col, off):
        def im(b, h, i):
            return (b * nb + jnp.clip(i + off, 0, nb - 1), col + h)
        return pl.BlockSpec((blk, pw), im)

    def ctx(col):
        return pl.BlockSpec((lc, pw), lambda b, h, i: (b, col + h))

    return pl.pallas_call(
        functools.partial(_na_kernel, rows=rows),
        grid=(bsz, hp, nb),
        in_specs=[lat(cq, 0), lat(ck, -1), lat(ck, 0), lat(ck, 1), lat(cv, -1), lat(cv, 0), lat(cv, 1),
                  ctx(ck), ctx(cv),
                  pl.BlockSpec((2, NA_KR, w, NA_KR * w), lambda b, h, i: (h, 0, 0, 0))],
        out_specs=pl.BlockSpec((blk, pw), lambda b, h, i: (b * nb + i, h)),
        out_shape=jax.ShapeDtypeStruct((bsz * seq, NA_WIDTH), BF16),
        scratch_shapes=[pltpu.VMEM((3 * blk, pw), BF16), pltpu.VMEM((3 * blk, pw), BF16)],
        compiler_params=_cparams("parallel", "parallel", "arbitrary"),
        name="neighborhood_attention",
    )(p_lat, p_lat, p_lat, p_lat, p_lat, p_lat, p_lat, p_ctx, p_ctx, bias_tbl)


def _ctx_attn_kernel(q_ref, k_ref, v_ref, o_ref):
    scale = NA_DH ** -0.5
    outs = []
    for hh in range(2):
        cs = slice(hh * NA_DH, (hh + 1) * NA_DH)
        q = (q_ref[:, cs] * scale).astype(BF16)
        s = _dot_nt(q, k_ref[:, cs].astype(BF16))
        p = jnp.exp(s - jnp.max(s, axis=-1, keepdims=True))
        o = _dot(p.astype(BF16), v_ref[:, cs].astype(BF16))
        outs.append(o / jnp.sum(p, axis=-1, keepdims=True))
    o_ref[...] = jnp.concatenate(outs, axis=-1).astype(o_ref.dtype)


def context_attention(p_ctx, *, bsz, lc, col0):
    hp = NA_HEADS // 2
    pw = 2 * NA_DH
    cq = col0 // pw

    def spec(col):
        return pl.BlockSpec((lc, pw), lambda b, h: (b, col + h))

    return pl.pallas_call(
        _ctx_attn_kernel,
        grid=(bsz, hp),
        in_specs=[spec(cq), spec(cq + hp), spec(cq + 2 * hp)],
        out_specs=pl.BlockSpec((lc, pw), lambda b, h: (b, h)),
        out_shape=jax.ShapeDtypeStruct((bsz * lc, NA_WIDTH), BF16),
        compiler_params=_cparams("parallel", "parallel"),
        name="context_attention",
    )(p_ctx, p_ctx, p_ctx)


def _residual_router(x, y, gate, g1, sh, sc, wrt, x1_ref, acc_ref, aff_ref):
    x1 = x + gate * y
    x1_ref[...] = x1
    acc_ref[...] = x1
    f = _rms_mod(x1, g1, sh, sc)
    lt = _dot3_nt(wrt, f)
    e = jnp.exp(lt - jnp.max(lt, axis=0, keepdims=True))
    aff_ref[0] = e / jnp.sum(e, axis=0, keepdims=True)


def _ab_out_kernel(of_ref, ob_ref, gg_ref, nl_ref, on_ref, w_ref, x_ref, gate_ref, g1_ref, sh_ref, sc_ref, wrt_ref,
                   x1_ref, acc_ref, aff_ref):
    o = of_ref[...] + ob_ref[...]
    gain = on_ref[...]
    parts = []
    for h in range(HGRN_HEADS):
        oh = o[:, h * HGRN_DK:(h + 1) * HGRN_DK]
        parts.append(oh * lax.rsqrt(jnp.mean(oh * oh, axis=-1, keepdims=True) + RMS_EPS) * gain)
    hl = (jnp.concatenate(parts, axis=-1) * _silu(gg_ref[...])).astype(BF16)
    y = _dot(hl, w_ref[0:HGRN_WIDTH, :]) + _dot(nl_ref[...], w_ref[HGRN_WIDTH:, :])
    _residual_router(x_ref[...], y, gate_ref[0], g1_ref[...], sh_ref[0], sc_ref[0], wrt_ref[...],
                     x1_ref, acc_ref, aff_ref)


def _ssd_out_kernel(yf_ref, yb_ref, xs_ref, z_ref, dsk_ref, ng_ref, w_ref, x_ref, gate_ref, g1_ref, sh_ref, sc_ref,
                    wrt_ref, x1_ref, acc_ref, aff_ref, *, group_width):
    y = (yf_ref[...] + yb_ref[...] + dsk_ref[...] * xs_ref[...]) * _silu(z_ref[...])
    parts = []
    for g in range(y.shape[1] // group_width):
        yg = y[:, g * group_width:(g + 1) * group_width]
        parts.append(yg * lax.rsqrt(jnp.mean(yg * yg, axis=-1, keepdims=True) + RMS_EPS))
    yn = (jnp.concatenate(parts, axis=-1) * ng_ref[...]).astype(BF16)
    _residual_router(x_ref[...], _dot(yn, w_ref[...]), gate_ref[0], g1_ref[...], sh_ref[0], sc_ref[0], wrt_ref[...],
                     x1_ref, acc_ref, aff_ref)


def _out_common(x, gate, g1, sh, sc, wrt, *, rows_per_mod, tm):
    r, d = x.shape
    per = rows_per_mod // tm
    e = wrt.shape[0]
    nbatch = r // rows_per_mod
    row = lambda i: (i, 0)
    mod = lambda i: (i // per, 0, 0)
    in_specs = [pl.BlockSpec((tm, d), row), pl.BlockSpec((1, 1, d), mod), pl.BlockSpec((1, d), lambda i: (0, 0)),
                pl.BlockSpec((1, 1, d), mod), pl.BlockSpec((1, 1, d), mod), pl.BlockSpec((e, d), lambda i: (0, 0))]
    out_specs = [pl.BlockSpec((tm, d), row), pl.BlockSpec((tm, d), row),
                 pl.BlockSpec((1, e, tm), lambda i: (i // per, 0, i % per))]
    out_shape = [jax.ShapeDtypeStruct((r, d), F32), jax.ShapeDtypeStruct((r, d), F32),
                 jax.ShapeDtypeStruct((nbatch, e, rows_per_mod), F32)]
    args = (x, gate, g1.reshape(1, d), sh, sc, wrt)
    return in_specs, out_specs, out_shape, args


def ab_out(o_f, o_b, p, nl, onorm_g, w_out, x, gate, g1, sh, sc, wrt, *, rows_per_mod, tm):
    tm = min(tm, rows_per_mod)
    r = x.shape[0]
    wd = HGRN_WIDTH
    row = lambda i: (i, 0)
    c_in, c_out, c_shape, c_args = _out_common(x, gate, g1, sh, sc, wrt, rows_per_mod=rows_per_mod, tm=tm)
    in_specs = [pl.BlockSpec((tm, wd), row), pl.BlockSpec((tm, wd), row),
                pl.BlockSpec((tm, wd), lambda i: (i, 4)),
                pl.BlockSpec((tm, NA_WIDTH), row),
                pl.BlockSpec((1, HGRN_DK), lambda i: (0, 0)),
                pl.BlockSpec(w_out.shape, lambda i: (0, 0))] + c_in
    return pl.pallas_call(
        _ab_out_kernel, grid=(r // tm,), in_specs=in_specs, out_specs=c_out, out_shape=c_shape,
        compiler_params=_cparams("parallel"), name="ab_out",
    )(o_f, o_b, p, nl, onorm_g.reshape(1, HGRN_DK), w_out, *c_args)


def ssd_out(y_f, y_b, xa, p, dsk, ng, w_out, x, gate, g1, sh, sc, wrt, *, rows_per_mod, tm):
    tm = min(tm, rows_per_mod)
    r = x.shape[0]
    inner = w_out.shape[0]
    row = lambda i: (i, 0)
    c_in, c_out, c_shape, c_args = _out_common(x, gate, g1, sh, sc, wrt, rows_per_mod=rows_per_mod, tm=tm)
    in_specs = [pl.BlockSpec((tm, inner), row), pl.BlockSpec((tm, inner), row),
                pl.BlockSpec((tm, inner), row),
                pl.BlockSpec((tm, inner), row),
                pl.BlockSpec((1, inner), lambda i: (0, 0)), pl.BlockSpec((1, inner), lambda i: (0, 0)),
                pl.BlockSpec(w_out.shape, lambda i: (0, 0))] + c_in
    return pl.pallas_call(
        functools.partial(_ssd_out_kernel, group_width=inner // SSD_GROUPS),
        grid=(r // tm,), in_specs=in_specs, out_specs=c_out, out_shape=c_shape,
        compiler_params=_cparams("parallel"), name="ssd_out",
    )(y_f, y_b, xa, p, dsk.reshape(1, inner), ng.reshape(1, inner), w_out, *c_args)


F32_INF_BITS = 0x7F800000


def _lane_cumsum(src_ref, dst_ref, n_chunks):
    e = src_ref.shape[0]
    upper = _tri(LANES, True).astype(BF16)

    def body(j, off):
        ds = pl.ds(pl.multiple_of(j * LANES, LANES), LANES)
        c = _dot(src_ref[:, ds].astype(BF16), upper) + off
        dst_ref[:, ds] = c
        return c[:, LANES - 1:LANES]

    lax.fori_loop(0, n_chunks, body, jnp.zeros((e, 1), F32))


def _select_kernel(aff_ref, idx_ref, gate_ref, m_scr, c_scr, *, seq, cap, nph, kc):
    e = aff_ref.shape[1]
    n_chunks = seq // LANES
    bits = pltpu.bitcast(aff_ref[0], I32)

    def bisect(_, lohi):
        lo, hi = lohi
        mid = lo + lax.shift_right_logical(hi - lo, 1)
        cnt = jnp.sum((bits >= mid).astype(F32), axis=1, keepdims=True)
        ge = cnt >= cap
        return jnp.where(ge, mid, lo), jnp.where(ge, hi, mid)

    thr, _ = lax.fori_loop(0, 31, bisect, (jnp.zeros((e, 1), I32), jnp.full((e, 1), F32_INF_BITS, I32)))
    gt = bits > thr
    eq = bits == thr
    need = cap - jnp.sum(gt.astype(F32), axis=1, keepdims=True)
    m_scr[...] = eq.astype(F32)
    _lane_cumsum(m_scr, c_scr, n_chunks)
    sel = gt | (eq & (c_scr[...] <= need))
    m_scr[...] = sel.astype(F32)
    _lane_cumsum(m_scr, c_scr, n_chunks)

    sub_p = lax.broadcasted_iota(I32, (nph, kc), 0).astype(F32)
    sub_l = lax.broadcasted_iota(I32, (LANES, kc), 0).astype(F32)
    lane_t = lax.broadcasted_iota(I32, (1, kc), 1)
    inv = 1.0 / LANES
    for ex in range(e):
        def body(c, acc):
            ds = pl.ds(pl.multiple_of(c * kc, LANES), kc)
            pos = c_scr[ex:ex + 1, ds] - 1.0
            phi = jnp.floor(pos * inv)
            plo = pos - LANES * phi
            hit = (phi == sub_p) & (m_scr[ex:ex + 1, ds] > 0.0)
            t = (lane_t + c * kc).astype(F32)
            th = jnp.floor(t * inv)
            tl = t - LANES * th
            a1, a2, a3 = _split3(aff_ref[0, ex:ex + 1, ds])
            rows = [th, tl, a1.astype(F32), a2.astype(F32), a3.astype(F32)]
            lhs = jnp.concatenate([jnp.where(hit, r, 0.0) for r in rows], axis=0).astype(BF16)
            onehot = (plo == sub_l).astype(BF16)
            return acc + _dot_nt(lhs, onehot)

        acc = lax.fori_loop(0, seq // kc, body, jnp.zeros((5 * nph, LANES), F32))
        idx_ref[0, ex] = (acc[0:nph] * LANES + acc[nph:2 * nph]).astype(I32)
        gate_ref[0, ex] = acc[2 * nph:3 * nph] + acc[3 * nph:4 * nph] + acc[4 * nph:5 * nph]


def expert_select(aff_t, cap):
    bsz, e, seq = aff_t.shape
    nph = -(-cap // LANES)
    nph = -(-nph // SUBLANES) * SUBLANES
    kc = min(2048, seq)
    idx, gate = pl.pallas_call(
        functools.partial(_select_kernel, seq=seq, cap=cap, nph=nph, kc=kc),
        grid=(bsz,),
        in_specs=[pl.BlockSpec((1, e, seq), lambda b: (b, 0, 0))],
        out_specs=[pl.BlockSpec((1, e, nph, LANES), lambda b: (b, 0, 0, 0)),
                   pl.BlockSpec((1, e, nph, LANES), lambda b: (b, 0, 0, 0))],
        out_shape=[jax.ShapeDtypeStruct((bsz, e, nph, LANES), I32),
                   jax.ShapeDtypeStruct((bsz, e, nph, LANES), F32)],
        scratch_shapes=[pltpu.VMEM((e, seq), F32), pltpu.VMEM((e, seq), F32)],
        compiler_params=_cparams("parallel"),
        name="expert_select",
    )(aff_t)
    return idx.reshape(bsz, e, nph * LANES)[:, :, :cap], gate.reshape(bsz, e, nph * LANES)[:, :, :cap]


def _ffn_kernel(idx_ref, gate_ref, g1_ref, sh_ref, sc_ref, g5_ref, w1_ref, w3_ref, w2_ref, x1_hbm, acc_in,
                acc_hbm, xbuf, abuf, sems, *, seq, tr):
    del acc_in
    base = pl.program_id(0) * seq
    off = pl.program_id(2) * tr

    def token(r):
        return pl.ds(base + idx_ref[0, 0, off + r], 1)

    def gather_x(r):
        return pltpu.make_async_copy(x1_hbm.at[token(r)], xbuf.at[pl.ds(r, 1)], sems.at[0])

    def gather_acc(r):
        return pltpu.make_async_copy(acc_hbm.at[token(r)], abuf.at[pl.ds(r, 1)], sems.at[1])

    def scatter_acc(r):
        return pltpu.make_async_copy(abuf.at[pl.ds(r, 1)], acc_hbm.at[token(r)], sems.at[2])

    def start_in(r, c):
        gather_x(r).start()
        gather_acc(r).start()
        return c

    def wait_in(r, c):
        gather_x(r).wait()
        gather_acc(r).wait()
        return c

    def start_out(r, c):
        scatter_acc(r).start()
        return c

    def wait_out(r, c):
        scatter_acc(r).wait()
        return c

    lax.fori_loop(0, tr, start_in, 0)
    lax.fori_loop(0, tr, wait_in, 0)
    f = _rms_mod(xbuf[...], g1_ref[...], sh_ref[0], sc_ref[0]).astype(BF16)
    hid = (_silu(_dot(f, w1_ref[0])) * _dot(f, w3_ref[0])).astype(BF16)
    y = _dot(hid, w2_ref[0])
    eye = lax.broadcasted_iota(I32, (tr, tr), 0) == lax.broadcasted_iota(I32, (tr, tr), 1)
    gcol = jnp.sum(jnp.where(eye, gate_ref[0], 0.0), axis=1, keepdims=True)
    abuf[...] = abuf[...] + (g5_ref[0] * gcol) * y
    lax.fori_loop(0, tr, start_out, 0)
    lax.fori_loop(0, tr, wait_out, 0)


def expert_ffn_combine(x1, acc, idx, gate, g1, sh, sc, g5, w1, w3, w2, *, seq):
    bsz, e, cap = idx.shape
    d = x1.shape[1]
    ff = w1.shape[2]
    tr = min(256, cap)
    nt = cap // tr
    mod = lambda b, ex, j: (b, 0, 0)
    return pl.pallas_call(
        functools.partial(_ffn_kernel, seq=seq, tr=tr),
        grid=(bsz, e, nt),
        in_specs=[pl.BlockSpec((1, 1, cap), lambda b, ex, j: (b * e + ex, 0, 0), memory_space=pltpu.SMEM),
                  pl.BlockSpec((1, 1, tr), lambda b, ex, j: ((b * e + ex) * nt + j, 0, 0)),
                  pl.BlockSpec((1, d), lambda b, ex, j: (0, 0)),
                  pl.BlockSpec((1, 1, d), mod), pl.BlockSpec((1, 1, d), mod), pl.BlockSpec((1, 1, d), mod),
                  pl.BlockSpec((1, d, ff), lambda b, ex, j: (ex, 0, 0)),
                  pl.BlockSpec((1, d, ff), lambda b, ex, j: (ex, 0, 0)),
                  pl.BlockSpec((1, ff, d), lambda b, ex, j: (ex, 0, 0)),
                  pl.BlockSpec(memory_space=pl.ANY),
                  pl.BlockSpec(memory_space=pl.ANY)],
        out_specs=pl.BlockSpec(memory_space=pl.ANY),
        out_shape=jax.ShapeDtypeStruct(acc.shape, F32),
        scratch_shapes=[pltpu.VMEM((tr, d), F32), pltpu.VMEM((tr, d), F32), pltpu.SemaphoreType.DMA((3,))],
        input_output_aliases={10: 0},
        compiler_params=_cparams("arbitrary", "arbitrary", "arbitrary"),
        name="expert_ffn_combine",
    )(idx.reshape(bsz * e, 1, cap), gate.reshape(bsz * e * nt, 1, tr), g1.reshape(1, d), sh, sc, g5,
      w1, w3, w2, x1, acc)


CONV_HALO = 8


def _ssd_prep_kernel(cur_ref, prev_ref, next_ref, w_ref, b_ref, dt_ref, dtb_ref, xa_ref, dtv_ref, ext_ref,
                     *, tm, per):
    i = pl.program_id(0)
    first = (i % per) == 0
    last = (i % per) == per - 1
    ext_ref[0:CONV_HALO, :] = jnp.where(first, 0.0, prev_ref[...])
    ext_ref[CONV_HALO:CONV_HALO + tm, :] = cur_ref[...]
    ext_ref[CONV_HALO + tm:2 * CONV_HALO + tm, :] = jnp.where(last, 0.0, next_ref[...])
    acc = b_ref[...] + w_ref[0:1, :] * ext_ref[pl.ds(CONV_HALO - SSD_CONV // 2, tm), :]
    for k in range(1, SSD_CONV):
        acc = acc + w_ref[k:k + 1, :] * ext_ref[pl.ds(CONV_HALO - SSD_CONV // 2 + k, tm), :]
    xa_ref[...] = _silu(acc)
    v = dt_ref[...] + dtb_ref[...]
    dtv_ref[...] = jnp.maximum(v, 0.0) + jnp.log1p(jnp.exp(-jnp.abs(v)))


def ssd_prep(p, conv_w, conv_b, dt_bias, *, rows_per_seq, tm, col_xbc, col_dt):
    r = p.shape[0]
    cdim = conv_w.shape[1]
    half = cdim // 2
    tm = min(tm, rows_per_seq)
    per = rows_per_seq // tm
    hb = tm // CONV_HALO
    nhb = r // CONV_HALO
    c0 = col_xbc // half
    return pl.pallas_call(
        functools.partial(_ssd_prep_kernel, tm=tm, per=per),
        grid=(r // tm, 2),
        in_specs=[pl.BlockSpec((tm, half), lambda i, j: (i, c0 + j)),
                  pl.BlockSpec((CONV_HALO, half), lambda i, j: (jnp.maximum(i * hb - 1, 0), c0 + j)),
                  pl.BlockSpec((CONV_HALO, half), lambda i, j: (jnp.minimum((i + 1) * hb, nhb - 1), c0 + j)),
                  pl.BlockSpec((SSD_CONV, half), lambda i, j: (0, j)),
                  pl.BlockSpec((1, half), lambda i, j: (0, j)),
                  pl.BlockSpec((tm, LANES), lambda i, j: (i, col_dt // LANES)),
                  pl.BlockSpec((1, LANES), lambda i, j: (0, 0))],
        out_specs=[pl.BlockSpec((tm, half), lambda i, j: (i, j)),
                   pl.BlockSpec((tm, LANES), lambda i, j: (i, 0))],
        out_shape=[jax.ShapeDtypeStruct((r, cdim), F32), jax.ShapeDtypeStruct((r, LANES), F32)],
        scratch_shapes=[pltpu.VMEM((tm + 2 * CONV_HALO, half), F32)],
        compiler_params=_cparams("parallel", "arbitrary"),
        name="ssd_prep",
    )(p, p, p, conv_w, conv_b.reshape(1, cdim), p, dt_bias)


SSD_HPG = 4


def _ssd_chunk(x, bm, cm, dtv, a_row, hs, lane0, upper):
    n = x.shape[0]
    hp = x.shape[1]
    pdim = hp // SSD_HPG
    mask = _tri(n, upper)
    acum = _dot_exact_lhs(mask.astype(F32), dtv * a_row)
    pick = ((lax.broadcasted_iota(I32, (SUBLANES, LANES), 1) ==
             lane0 + lax.broadcasted_iota(I32, (SUBLANES, LANES), 0)) &
            (lax.broadcasted_iota(I32, (SUBLANES, LANES), 0) < SSD_HPG)).astype(BF16)
    a3 = _split3(acum)
    d3 = _split3(dtv)
    a_col = sum(_dot_nt(t, pick) for t in a3)
    d_col = sum(_dot_nt(t, pick) for t in d3)
    a_rowf = sum(_dot_nt(pick, t) for t in a3)
    d_rowf = sum(_dot_nt(pick, t) for t in d3)
    cb = _dot_nt(cm.astype(BF16), bm.astype(BF16))
    xb = x.astype(BF16)
    head_of_lane = lax.broadcasted_iota(I32, (1, hp), 1) // pdim
    head_of_row = lax.broadcasted_iota(I32, (hp, 1), 0) // pdim
    y = jnp.zeros((n, hp), F32)
    e_all = jnp.zeros((n, hp), F32)
    w_all = jnp.zeros((n, hp), F32)
    d_all = jnp.zeros((hp, 1), F32)
    for h in range(SSD_HPG):
        ac = a_col[:, h:h + 1]
        ar = a_rowf[h:h + 1, :]
        alast = ac[0:1] if upper else ac[n - 1:n]
        dec = jnp.where(mask, jnp.exp(jnp.minimum(ac - ar, 0.0)), 0.0)
        g = (cb * dec * d_rowf[h:h + 1, :]).astype(BF16)
        hm = head_of_lane == h
        y = jnp.where(hm, _dot(g, xb), y)
        e_all = jnp.where(hm, jnp.exp(ac), e_all)
        w_all = jnp.where(hm, jnp.exp(alast - ac) * d_col[:, h:h + 1], w_all)
        d_all = jnp.where(head_of_row == h, jnp.exp(alast), d_all)
    y = y + _dot_nt(cm.astype(BF16), hs.astype(BF16)) * e_all
    hs = hs * d_all + _dot_tn((x * w_all).astype(BF16), bm.astype(BF16))
    return y, hs


def _ssd_kernel(xf_ref, bf_ref, cf_ref, df_ref, xb_ref, bb_ref, cb_ref, db_ref,
                xc_ref, bc_ref, cc_ref, dc_ref, alog_ref, yf_ref, yb_ref, hf_ref, hb_ref, *, tb, lc):
    ch = SSD_CHUNK
    g = pl.program_id(0) % SSD_GROUPS
    a_row = -jnp.exp(alog_ref[...])
    lane_f = g * SSD_HPG
    lane_b = SSD_GROUPS * SSD_HPG + g * SSD_HPG

    @pl.when(pl.program_id(1) == 0)
    def _():
        ncc = lc // ch

        def cbody(c, carry):
            hf, hb = carry
            rf = pl.ds(pl.multiple_of(c * ch, ch), ch)
            rb = pl.ds(pl.multiple_of((ncc - 1 - c) * ch, ch), ch)
            _, hf = _ssd_chunk(xc_ref[rf, :], bc_ref[rf, :], cc_ref[rf, :], dc_ref[rf, :], a_row, hf, lane_f, False)
            _, hb = _ssd_chunk(xc_ref[rb, :], bc_ref[rb, :], cc_ref[rb, :], dc_ref[rb, :], a_row, hb, lane_b, True)
            return hf, hb

        z0 = jnp.zeros(hf_ref.shape, F32)
        hf, hb = lax.fori_loop(0, ncc, cbody, (z0, z0))
        hf_ref[...] = hf
        hb_ref[...] = hb

    nc = tb // ch

    def body(c, carry):
        hf, hb = carry
        rf = pl.ds(pl.multiple_of(c * ch, ch), ch)
        rb = pl.ds(pl.multiple_of((nc - 1 - c) * ch, ch), ch)
        y, hf = _ssd_chunk(xf_ref[rf, :], bf_ref[rf, :], cf_ref[rf, :], df_ref[rf, :], a_row, hf, lane_f, False)
        yf_ref[rf, :] = y
        y, hb = _ssd_chunk(xb_ref[rb, :], bb_ref[rb, :], cb_ref[rb, :], db_ref[rb, :], a_row, hb, lane_b, True)
        yb_ref[rb, :] = y
        return hf, hb

    hf, hb = lax.fori_loop(0, nc, body, (hf_ref[...], hb_ref[...]))
    hf_ref[...] = hf
    hb_ref[...] = hb


def ssd_scan(xa, dtv, xa_c, dtv_c, a_log_row, *, bsz, seq, lc, tb, inner):
    g = SSD_GROUPS
    n = SSD_STATE
    xw = inner // g
    nb = seq // tb
    cb0 = inner // n
    cc0 = cb0 + g

    def blk(i, rev):
        return nb - 1 - i if rev else i

    def lat(width, col, rev):
        return pl.BlockSpec((tb, width), lambda bg, i: ((bg // g) * nb + blk(i, rev), col + bg % g))

    def lat_dt(rev):
        return pl.BlockSpec((tb, LANES), lambda bg, i: ((bg // g) * nb + blk(i, rev), 0))

    def ctx(width, col):
        return pl.BlockSpec((lc, width), lambda bg, i: (bg // g, col + bg % g))

    def out(rev):
        return pl.BlockSpec((tb, xw), lambda bg, i: ((bg // g) * nb + blk(i, rev), bg % g))

    return pl.pallas_call(
        functools.partial(_ssd_kernel, tb=tb, lc=lc),
        grid=(bsz * g, nb),
        in_specs=[lat(xw, 0, False), lat(n, cb0, False), lat(n, cc0, False), lat_dt(False),
                  lat(xw, 0, True), lat(n, cb0, True), lat(n, cc0, True), lat_dt(True),
                  ctx(xw, 0), ctx(n, cb0), ctx(n, cc0), pl.BlockSpec((lc, LANES), lambda bg, i: (bg // g, 0)),
                  pl.BlockSpec((1, LANES), lambda bg, i: (0, 0))],
        out_specs=[out(False), out(True)],
        out_shape=[jax.ShapeDtypeStruct((bsz * seq, inner), F32), jax.ShapeDtypeStruct((bsz * seq, inner), F32)],
        scratch_shapes=[pltpu.VMEM((xw, n), F32), pltpu.VMEM((xw, n), F32)],
        compiler_params=_cparams("parallel", "arbitrary"),
        name="ssd_scan",
    )(xa, xa, xa, dtv, xa, xa, xa, dtv, xa_c, xa_c, xa_c, dtv_c, a_log_row)


def _final_norm_kernel(x_ref, g_ref, o_ref):
    x = x_ref[...]
    o_ref[...] = x * lax.rsqrt(jnp.mean(x * x, axis=-1, keepdims=True) + RMS_EPS) * g_ref[...]


def final_norm(x, g, *, tm):
    r, d = x.shape
    return pl.pallas_call(
        _final_norm_kernel,
        grid=(r // tm,),
        in_specs=[pl.BlockSpec((tm, d), lambda i: (i, 0)), pl.BlockSpec((1, d), lambda i: (0, 0))],
        out_specs=pl.BlockSpec((tm, d), lambda i: (i, 0)),
        out_shape=jax.ShapeDtypeStruct((r, d), F32),
        compiler_params=_cparams("parallel"),
        name="final_norm",
    )(x, g.reshape(1, d))


def _moe(x1, acc, aff, g1, sh, sc, g5, w1, w3, w2, *, seq):
    cap = max(1, CAPACITY_FACTOR * seq // N_EXPERTS)
    idx, gate = expert_select(aff, cap)
    return expert_ffn_combine(x1, acc, idx, gate, g1, sh, sc, g5, w1, w3, w2, seq=seq)


def kernel(x, c, ctx, c_ctx, ada_w, ada_b, norm_g, final_g, ab_w_in, ab_w_out, hgrn_lb_logits, hgrn_onorm_g, na_rpb, ssd_w_in, ssd_conv_w, ssd_conv_b, ssd_a_log, ssd_dt_bias, ssd_d, ssd_norm_g, ssd_w_out, moe_router, moe_w1, moe_w3, moe_w2):
    bsz, seq, d = x.shape
    lc = ctx.shape[1]
    depth = ada_w.shape[0]
    assert depth == 2, "context outputs of an SSD layer are not implemented (only needed when a layer follows it)"
    mods = modulation_table(c, c_ctx, ada_w, ada_b)
    lb_all = jnp.cumsum(jax.nn.softmax(hgrn_lb_logits.astype(F32), axis=1), axis=1)
    xl = x.reshape(bsz * seq, d)
    xc = ctx.reshape(bsz * lc, d)
    hg = 5 * HGRN_WIDTH
    for l in range(depth):
        need_ctx = l < depth - 1
        k = l // 2
        m = mods[l]
        ml = [m[:bsz, j * d:(j + 1) * d].reshape(bsz, 1, d) for j in range(6)]
        mc = [jnp.broadcast_to(m[bsz, j * d:(j + 1) * d].reshape(1, 1, d), (bsz, 1, d)) for j in range(6)]
        wrt = moe_router[l].T
        w1, w3, w2 = (w[l].astype(BF16) for w in (moe_w1, moe_w3, moe_w2))
        g0, g1 = norm_g[l, 0], norm_g[l, 1]
        if l % 2 == 0:
            w_in = ab_w_in[k].astype(BF16)
            w_out = ab_w_out[k].astype(BF16)
            p_lat = norm_mod_project(xl, g0, ml[0], ml[1], w_in, rows_per_mod=seq, tm=512, tn=1024)
            p_ctx = norm_mod_project(xc, g0, mc[0], mc[1], w_in, rows_per_mod=lc, tm=512, tn=1024)
            o_f, o_b, oc_f, oc_b = hgrn_scan(p_lat, p_ctx, lb_all[:, k], bsz=bsz, seq=seq, lc=lc, tb=min(256, seq))
            nl = neighborhood_attention(p_lat, p_ctx, na_bias_table(na_rpb[k]), bsz=bsz, seq=seq, lc=lc, col0=hg)
            x1, acc, aff = ab_out(o_f, o_b, p_lat, nl, hgrn_onorm_g[k], w_out, xl, ml[2], g1, ml[3], ml[4], wrt,
                                  rows_per_mod=seq, tm=512)
            if need_ctx:
                nc = context_attention(p_ctx, bsz=bsz, lc=lc, col0=hg)
                c1, cacc, caff = ab_out(oc_f, oc_b, p_ctx, nc, hgrn_onorm_g[k], w_out, xc, mc[2], g1, mc[3], mc[4],
                                        wrt, rows_per_mod=lc, tm=512)
        else:
            inner = ssd_w_out.shape[1]
            cdim = ssd_conv_w.shape[2]
            heads2 = ssd_dt_bias.shape[1] * ssd_dt_bias.shape[2]
            wz = ssd_w_in[k]
            pad = jnp.zeros((d, LANES - heads2), F32)
            w_in = jnp.concatenate([wz, pad], axis=1).astype(BF16)
            w_out = ssd_w_out[k].astype(BF16)
            tn = w_in.shape[1] // 7
            p_lat = norm_mod_project(xl, g0, ml[0], ml[1], w_in, rows_per_mod=seq, tm=512, tn=tn)
            p_ctx = norm_mod_project(xc, g0, mc[0], mc[1], w_in, rows_per_mod=lc, tm=512, tn=tn)
            dtb = jnp.zeros((1, LANES), F32).at[0, :heads2].set(ssd_dt_bias[k].reshape(-1))
            alog = jnp.zeros((1, LANES), F32).at[0, :heads2].set(ssd_a_log[k].reshape(-1))
            prep = functools.partial(ssd_prep, conv_w=ssd_conv_w[k], conv_b=ssd_conv_b[k], dt_bias=dtb, tm=256,
                                     col_xbc=inner, col_dt=inner + cdim)
            xa, dtv = prep(p_lat, rows_per_seq=seq)
            xa_c, dtv_c = prep(p_ctx, rows_per_seq=lc)
            y_f, y_b = ssd_scan(xa, dtv, xa_c, dtv_c, alog, bsz=bsz, seq=seq, lc=lc, tb=min(256, seq), inner=inner)
            dsk = jnp.repeat(ssd_d[k].astype(F32), SSD_HEADDIM)
            x1, acc, aff = ssd_out(y_f, y_b, xa, p_lat, dsk, ssd_norm_g[k], w_out, xl, ml[2], g1, ml[3], ml[4], wrt,
                                   rows_per_mod=seq, tm=256)
        xl = _moe(x1, acc, aff, g1, ml[3], ml[4], ml[5], w1, w3, w2, seq=seq)
        if need_ctx:
            xc = _moe(c1, cacc, caff, g1, mc[3], mc[4], mc[5], w1, w3, w2, seq=lc)
    return final_norm(xl, final_g, tm=512).reshape(bsz, seq, d)
```

```python
import functools
import math

import jax
import jax.numpy as jnp
import numpy as np
from jax import lax
from jax.experimental import pallas as pl
from jax.experimental.pallas import tpu as pltpu

F32 = jnp.float32
BF16 = jnp.bfloat16
I32 = jnp.int32

GRID_W = 64
HGRN_HEADS = 4
HGRN_DK = 128
HGRN_WIDTH = HGRN_HEADS * HGRN_DK
GLA_CHUNK = 32
NA_HEADS = 8
NA_DH = 64
NA_WIDTH = NA_HEADS * NA_DH
NA_KR = 8
NA_KC = 16
SSD_HEADDIM = 64
SSD_GROUPS = 8
SSD_STATE = 128
SSD_CONV = 5
SSD_CHUNK = 64
N_EXPERTS = 16
CAPACITY_FACTOR = 2
RMS_EPS = 1e-6
NEG_INF = -1e30

LANES = 128
SUBLANES = 8
VMEM_LIMIT_BYTES = 56 * 1024 * 1024


def _cparams(*sem):
    return pltpu.CompilerParams(dimension_semantics=sem, vmem_limit_bytes=VMEM_LIMIT_BYTES)


def _sigmoid(x):
    return 1.0 / (1.0 + jnp.exp(-x))


def _silu(x):
    return x * _sigmoid(x)


def _split3(x):
    a = x.astype(BF16)
    r = x - a.astype(F32)
    b = r.astype(BF16)
    c = (r - b.astype(F32)).astype(BF16)
    return a, b, c


def _dot(a, b):
    return jnp.dot(a, b, preferred_element_type=F32)


def _dot_nt(a, b):
    return lax.dot_general(a, b, (((1,), (1,)), ((), ())), preferred_element_type=F32)


def _dot_tn(a, b):
    return lax.dot_general(a, b, (((0,), (0,)), ((), ())), preferred_element_type=F32)


def _dot_exact_lhs(m01, x):
    m = m01.astype(BF16)
    a, b, c = _split3(x)
    return _dot(m, a) + _dot(m, b) + _dot(m, c)


def _dot3(a, b):
    a1 = a.astype(BF16)
    a2 = (a - a1.astype(F32)).astype(BF16)
    b1 = b.astype(BF16)
    b2 = (b - b1.astype(F32)).astype(BF16)
    return _dot(a1, b1) + _dot(a1, b2) + _dot(a2, b1)


def _dot3_nt(a, b):
    a1 = a.astype(BF16)
    a2 = (a - a1.astype(F32)).astype(BF16)
    b1 = b.astype(BF16)
    b2 = (b - b1.astype(F32)).astype(BF16)
    return _dot_nt(a1, b1) + _dot_nt(a1, b2) + _dot_nt(a2, b1)


def _rms_mod(x, g, shift, scale):
    y = x * lax.rsqrt(jnp.mean(x * x, axis=-1, keepdims=True) + RMS_EPS) * g
    return y * (1.0 + scale) + shift


def _mod_kernel(s_ref, w_ref, b_ref, o_ref):
    s = _silu(s_ref[...])
    o_ref[0] = _dot3(s, w_ref[0]) + b_ref[0]


def modulation_table(c, c_ctx, ada_w, ada_b):
    depth, d, n = ada_w.shape
    bsz = c.shape[0]
    s = jnp.zeros((SUBLANES, d), F32).at[:bsz].set(c).at[bsz].set(c_ctx)
    tn = 1024
    return pl.pallas_call(
        _mod_kernel,
        grid=(depth, n // tn),
        in_specs=[pl.BlockSpec((SUBLANES, d), lambda l, j: (0, 0)),
                  pl.BlockSpec((1, d, tn), lambda l, j: (l, 0, j)),
                  pl.BlockSpec((1, 1, tn), lambda l, j: (l, 0, j))],
        out_specs=pl.BlockSpec((1, SUBLANES, tn), lambda l, j: (l, 0, j)),
        out_shape=jax.ShapeDtypeStruct((depth, SUBLANES, n), F32),
        compiler_params=_cparams("parallel", "parallel"),
        name="mod_table",
    )(s, ada_w, ada_b.reshape(depth, 1, n))


def _proj_kernel(x_ref, g_ref, sh_ref, sc_ref, w_ref, o_ref, *, tn):
    h = _rms_mod(x_ref[...], g_ref[...], sh_ref[0], sc_ref[0]).astype(BF16)
    for j in range(w_ref.shape[1] // tn):
        o_ref[:, j * tn:(j + 1) * tn] = _dot(h, w_ref[:, j * tn:(j + 1) * tn])


def norm_mod_project(x, g, shift, scale, w, *, rows_per_mod, tm, tn, xcol=0):
    r = x.shape[0]
    d, n = w.shape
    tm = min(tm, rows_per_mod)
    per = rows_per_mod // tm
    return pl.pallas_call(
        functools.partial(_proj_kernel, tn=tn),
        grid=(r // tm,),
        in_specs=[pl.BlockSpec((tm, d), lambda i: (i, xcol)),
                  pl.BlockSpec((1, d), lambda i: (0, 0)),
                  pl.BlockSpec((1, 1, d), lambda i: (i // per, 0, 0)),
                  pl.BlockSpec((1, 1, d), lambda i: (i // per, 0, 0)),
                  pl.BlockSpec((d, n), lambda i: (0, 0))],
        out_specs=pl.BlockSpec((tm, n), lambda i: (i, 0)),
        out_shape=jax.ShapeDtypeStruct((r, n), F32),
        compiler_params=_cparams("parallel"),
        name="norm_mod_project",
    )(x, g.reshape(1, d), shift, scale, w)


def _tri(n, upper):
    r = lax.broadcasted_iota(I32, (n, n), 0)
    c = lax.broadcasted_iota(I32, (n, n), 1)
    return (r <= c) if upper else (r >= c)


def _block_tri(n, chunk, upper):
    r = lax.broadcasted_iota(I32, (n, n), 0)
    c = lax.broadcasted_iota(I32, (n, n), 1)
    same = (r // chunk) == (c // chunk)
    return same & ((r <= c) if upper else (r >= c))


def _hgrn_block(q, z, v, lb, st, upper):
    n = q.shape[0]
    ch = GLA_CHUNK
    nc = n // ch
    mask = _block_tri(n, ch, upper)
    qs = _silu(q)
    f = lb + (1.0 - lb) * _sigmoid(z)
    g = jnp.log(f)
    k = 1.0 - f
    c3 = _dot(mask.astype(BF16), jnp.concatenate(_split3(g), axis=1))
    dk = g.shape[1]
    cum = c3[:, 0:dk] + c3[:, dk:2 * dk] + c3[:, 2 * dk:3 * dk]
    cls = [cum[c * ch:c * ch + 1] if upper else cum[(c + 1) * ch - 1:(c + 1) * ch] for c in range(nc)]
    clb = jnp.concatenate([jnp.broadcast_to(cl, (ch, dk)) for cl in cls], axis=0)
    q_in = (qs * jnp.exp(cum)).astype(BF16)
    k_in = (k * jnp.exp(-cum)).astype(BF16)
    k_out = (k * jnp.exp(clb - cum)).astype(BF16)
    vb = v.astype(BF16)
    attn = jnp.where(mask, _dot_nt(q_in, k_in), 0.0)
    intra = _dot(attn.astype(BF16), vb)
    outs = [None] * nc
    for c in (reversed(range(nc)) if upper else range(nc)):
        rs = slice(c * ch, (c + 1) * ch)
        outs[c] = intra[rs] + _dot_nt(q_in[rs], st.astype(BF16))
        st = st * jnp.exp(cls[c]) + _dot_tn(vb[rs], k_out[rs])
    return jnp.concatenate(outs, axis=0), st


def _hgrn_kernel(qf_ref, zf_ref, vf_ref, qb_ref, zb_ref, vb_ref,
                 qc_ref, zfc_ref, zbc_ref, vc_ref, lb_ref,
                 of_ref, ob_ref, ocf_ref, ocb_ref, stf_ref, stb_ref):
    lbf = lb_ref[0:1]
    lbb = lb_ref[1:2]

    @pl.when(pl.program_id(1) == 0)
    def _():
        z0 = jnp.zeros((HGRN_DK, HGRN_DK), F32)
        ocf_ref[...], stf_ref[...] = _hgrn_block(qc_ref[...], zfc_ref[...], vc_ref[...], lbf, z0, False)
        ocb_ref[...], stb_ref[...] = _hgrn_block(qc_ref[...], zbc_ref[...], vc_ref[...], lbb, z0, True)

    of_ref[...], stf_ref[...] = _hgrn_block(qf_ref[...], zf_ref[...], vf_ref[...], lbf, stf_ref[...], False)
    ob_ref[...], stb_ref[...] = _hgrn_block(qb_ref[...], zb_ref[...], vb_ref[...], lbb, stb_ref[...], True)


def hgrn_scan(p_lat, p_ctx, lb, *, bsz, seq, lc, tb):
    h = HGRN_HEADS
    dk = HGRN_DK
    nb = seq // tb
    cq, czf, czb, cv = 0, h, 2 * h, 3 * h

    def lat(col, rev):
        if rev:
            return pl.BlockSpec((tb, dk), lambda bh, i: ((bh // h) * nb + nb - 1 - i, col + bh % h))
        return pl.BlockSpec((tb, dk), lambda bh, i: ((bh // h) * nb + i, col + bh % h))

    def ctx(col):
        return pl.BlockSpec((lc, dk), lambda bh, i: (bh // h, col + bh % h))

    def out_lat(rev):
        if rev:
            return pl.BlockSpec((tb, dk), lambda bh, i: ((bh // h) * nb + nb - 1 - i, bh % h))
        return pl.BlockSpec((tb, dk), lambda bh, i: ((bh // h) * nb + i, bh % h))

    out_ctx = pl.BlockSpec((lc, dk), lambda bh, i: (bh // h, bh % h))
    w = h * dk
    return pl.pallas_call(
        _hgrn_kernel,
        grid=(bsz * h, nb),
        in_specs=[lat(cq, False), lat(czf, False), lat(cv, False),
                  lat(cq, True), lat(czb, True), lat(cv, True),
                  ctx(cq), ctx(czf), ctx(czb), ctx(cv),
                  pl.BlockSpec((2, dk), lambda bh, i: (0, bh % h))],
        out_specs=[out_lat(False), out_lat(True), out_ctx, out_ctx],
        out_shape=[jax.ShapeDtypeStruct((bsz * seq, w), F32), jax.ShapeDtypeStruct((bsz * seq, w), F32),
                   jax.ShapeDtypeStruct((bsz * lc, w), F32), jax.ShapeDtypeStruct((bsz * lc, w), F32)],
        scratch_shapes=[pltpu.VMEM((dk, dk), F32), pltpu.VMEM((dk, dk), F32)],
        compiler_params=_cparams("parallel", "arbitrary"),
        name="hgrn_scan",
    )(p_lat, p_lat, p_lat, p_lat, p_lat, p_lat, p_ctx, p_ctx, p_ctx, p_ctx, lb)


NA_RB = 8


def na_bias_table(rpb):
    w = GRID_W
    qc = np.arange(w)
    win0 = np.clip(qc - NA_KC // 2, 0, w - NA_KC)
    kc = np.arange(w)
    ok = (kc[None, :] >= win0[:, None]) & (kc[None, :] < win0[:, None] + NA_KC)
    dc = np.clip(kc[None, :] - qc[:, None] + NA_KC - 1, 0, 2 * NA_KC - 2)
    v = np.arange(NA_KR)
    j = np.arange(NA_KR)
    dr = j[None, :] - v[:, None] + NA_KR - 1
    t = rpb.astype(F32)[:, dr][:, :, :, dc]
    t = jnp.where(ok[None, None, None], t, NEG_INF)
    t = jnp.transpose(t, (0, 1, 3, 2, 4))
    return t.reshape(rpb.shape[0], NA_KR, w, NA_KR * w)


def _na_kernel(q_ref, kp_ref, kc_ref, kn_ref, vp_ref, vc_ref, vn_ref, kx_ref, vx_ref, bias_ref,
               o_ref, ks_ref, vs_ref, *, rows):
    w = GRID_W
    blk = NA_RB * w
    i = pl.program_id(2)
    ks_ref[0:blk, :] = kp_ref[...].astype(BF16)
    ks_ref[blk:2 * blk, :] = kc_ref[...].astype(BF16)
    ks_ref[2 * blk:3 * blk, :] = kn_ref[...].astype(BF16)
    vs_ref[0:blk, :] = vp_ref[...].astype(BF16)
    vs_ref[blk:2 * blk, :] = vc_ref[...].astype(BF16)
    vs_ref[2 * blk:3 * blk, :] = vn_ref[...].astype(BF16)
    kx = kx_ref[...].astype(BF16)
    vx = vx_ref[...].astype(BF16)
    scale = NA_DH ** -0.5
    for rr in range(NA_RB):
        r = i * NA_RB + rr
        r0 = jnp.clip(r - NA_KR // 2, 0, rows - NA_KR)
        start = pl.multiple_of((r0 - (i - 1) * NA_RB) * w, w)
        vv = r - r0
        outs = []
        for hh in range(2):
            cs = slice(hh * NA_DH, (hh + 1) * NA_DH)
            q = (q_ref[rr * w:(rr + 1) * w, cs] * scale).astype(BF16)
            kw = ks_ref[pl.ds(start, NA_KR * w), cs]
            vw = vs_ref[pl.ds(start, NA_KR * w), cs]
            s = _dot_nt(q, kw) + bias_ref[hh, vv]
            sx = _dot_nt(q, kx[:, cs])
            m = jnp.maximum(jnp.max(s, axis=-1, keepdims=True), jnp.max(sx, axis=-1, keepdims=True))
            p = jnp.exp(s - m)
            px = jnp.exp(sx - m)
            den = jnp.sum(p, axis=-1, keepdims=True) + jnp.sum(px, axis=-1, keepdims=True)
            o = _dot(p.astype(BF16), vw) + _dot(px.astype(BF16), vx[:, cs])
            outs.append(o / den)
        o_ref[rr * w:(rr + 1) * w, :] = jnp.concatenate(outs, axis=-1).astype(o_ref.dtype)


def neighborhood_attention(p_lat, p_ctx, bias_tbl, *, bsz, seq, lc, col0):
    w = GRID_W
    rows = seq // w
    blk = NA_RB * w
    nb = rows // NA_RB
    hp = NA_HEADS // 2
    pw = 2 * NA_DH
    cq = col0 // pw
    ck = cq + hp
    cv = ck + hp

    def lat(col, off):
        def im(b, h, i):
            return (b * nb + jnp.clip(i + off, 0, nb - 1), col + h)
        return pl.BlockSpec((blk, pw), im)

    def ctx(col):
        return pl.BlockSpec((lc, pw), lambda b, h, i: (b, col + h))

    return pl.pallas_call(
        functools.partial(_na_kernel, rows=rows),
        grid=(bsz, hp, nb),
        in_specs=[lat(cq, 0), lat(ck, -1), lat(ck, 0), lat(ck, 1), lat(cv, -1), lat(cv, 0), lat(cv, 1),
                  ctx(ck), ctx(cv),
                  pl.BlockSpec((2, NA_KR, w, NA_KR * w), lambda b, h, i: (h, 0, 0, 0))],
        out_specs=pl.BlockSpec((blk, pw), lambda b, h, i: (b * nb + i, h)),
        out_shape=jax.ShapeDtypeStruct((bsz * seq, NA_WIDTH), BF16),
        scratch_shapes=[pltpu.VMEM((3 * blk, pw), BF16), pltpu.VMEM((3 * blk, pw), BF16)],
        compiler_params=_cparams("parallel", "parallel", "arbitrary"),
        name="neighborhood_attention",
    )(p_lat, p_lat, p_lat, p_lat, p_lat, p_lat, p_lat, p_ctx, p_ctx, bias_tbl)


def _ctx_attn_kernel(q_ref, k_ref, v_ref, o_ref):
    scale = NA_DH ** -0.5
    outs = []
    for hh in range(2):
        cs = slice(hh * NA_DH, (hh + 1) * NA_DH)
        q = (q_ref[:, cs] * scale).astype(BF16)
        s = _dot_nt(q, k_ref[:, cs].astype(BF16))
        p = jnp.exp(s - jnp.max(s, axis=-1, keepdims=True))
        o = _dot(p.astype(BF16), v_ref[:, cs].astype(BF16))
        outs.append(o / jnp.sum(p, axis=-1, keepdims=True))
    o_ref[...] = jnp.concatenate(outs, axis=-1).astype(o_ref.dtype)


def context_attention(p_ctx, *, bsz, lc, col0):
    hp = NA_HEADS // 2
    pw = 2 * NA_DH
    cq = col0 // pw

    def spec(col):
        return pl.BlockSpec((lc, pw), lambda b, h: (b, col + h))

    return pl.pallas_call(
        _ctx_attn_kernel,
        grid=(bsz, hp),
        in_specs=[spec(cq), spec(cq + hp), spec(cq + 2 * hp)],
        out_specs=pl.BlockSpec((lc, pw), lambda b, h: (b, h)),
        out_shape=jax.ShapeDtypeStruct((bsz * lc, NA_WIDTH), BF16),
        compiler_params=_cparams("parallel", "parallel"),
        name="context_attention",
    )(p_ctx, p_ctx, p_ctx)


def _residual_router(x, y, gate, g1, sh, sc, wrt, xa_ref, aff_ref):
    x1 = x + gate * y
    d = x1.shape[1]
    xa_ref[:, 0:d] = x1
    xa_ref[:, d:2 * d] = x1
    f = _rms_mod(x1, g1, sh, sc)
    lt = _dot3_nt(wrt, f)
    e = jnp.exp(lt - jnp.max(lt, axis=0, keepdims=True))
    aff_ref[0] = e / jnp.sum(e, axis=0, keepdims=True)


def _ab_out_kernel(of_ref, ob_ref, gg_ref, nl_ref, on_ref, w_ref, x_ref, gate_ref, g1_ref, sh_ref, sc_ref, wrt_ref,
                   xa_ref, aff_ref):
    o = of_ref[...] + ob_ref[...]
    gain = on_ref[...]
    parts = []
    for h in range(HGRN_HEADS):
        oh = o[:, h * HGRN_DK:(h + 1) * HGRN_DK]
        parts.append(oh * lax.rsqrt(jnp.mean(oh * oh, axis=-1, keepdims=True) + RMS_EPS) * gain)
    hl = (jnp.concatenate(parts, axis=-1) * _silu(gg_ref[...])).astype(BF16)
    y = _dot(hl, w_ref[0:HGRN_WIDTH, :]) + _dot(nl_ref[...], w_ref[HGRN_WIDTH:, :])
    _residual_router(x_ref[...], y, gate_ref[0], g1_ref[...], sh_ref[0], sc_ref[0], wrt_ref[...],
                     xa_ref, aff_ref)


def _ssd_out_kernel(yf_ref, yb_ref, xs_ref, z_ref, dsk_ref, ng_ref, w_ref, x_ref, gate_ref, g1_ref, sh_ref, sc_ref,
                    wrt_ref, xa_ref, aff_ref, *, group_width):
    y = (yf_ref[...] + yb_ref[...] + dsk_ref[...] * xs_ref[...]) * _silu(z_ref[...])
    parts = []
    for g in range(y.shape[1] // group_width):
        yg = y[:, g * group_width:(g + 1) * group_width]
        parts.append(yg * lax.rsqrt(jnp.mean(yg * yg, axis=-1, keepdims=True) + RMS_EPS))
    yn = (jnp.concatenate(parts, axis=-1) * ng_ref[...]).astype(BF16)
    _residual_router(x_ref[...], _dot(yn, w_ref[...]), gate_ref[0], g1_ref[...], sh_ref[0], sc_ref[0], wrt_ref[...],
                     xa_ref, aff_ref)


def _out_common(x, gate, g1, sh, sc, wrt, *, rows_per_mod, tm, xcol):
    r = x.shape[0]
    e, d = wrt.shape
    per = rows_per_mod // tm
    nbatch = r // rows_per_mod
    row = lambda i: (i, 0)
    mod = lambda i: (i // per, 0, 0)
    in_specs = [pl.BlockSpec((tm, d), lambda i: (i, xcol)), pl.BlockSpec((1, 1, d), mod),
                pl.BlockSpec((1, d), lambda i: (0, 0)),
                pl.BlockSpec((1, 1, d), mod), pl.BlockSpec((1, 1, d), mod), pl.BlockSpec((e, d), lambda i: (0, 0))]
    out_specs = [pl.BlockSpec((tm, 2 * d), row), pl.BlockSpec((1, e, tm), lambda i: (i // per, 0, i % per))]
    out_shape = [jax.ShapeDtypeStruct((r, 2 * d), F32), jax.ShapeDtypeStruct((nbatch, e, rows_per_mod), F32)]
    args = (x, gate, g1.reshape(1, d), sh, sc, wrt)
    return in_specs, out_specs, out_shape, args


def ab_out(o_f, o_b, p, nl, onorm_g, w_out, x, gate, g1, sh, sc, wrt, *, rows_per_mod, tm, xcol):
    tm = min(tm, rows_per_mod)
    r = x.shape[0]
    wd = HGRN_WIDTH
    row = lambda i: (i, 0)
    c_in, c_out, c_shape, c_args = _out_common(x, gate, g1, sh, sc, wrt, rows_per_mod=rows_per_mod, tm=tm, xcol=xcol)
    in_specs = [pl.BlockSpec((tm, wd), row), pl.BlockSpec((tm, wd), row),
                pl.BlockSpec((tm, wd), lambda i: (i, 4)),
                pl.BlockSpec((tm, NA_WIDTH), row),
                pl.BlockSpec((1, HGRN_DK), lambda i: (0, 0)),
                pl.BlockSpec(w_out.shape, lambda i: (0, 0))] + c_in
    return pl.pallas_call(
        _ab_out_kernel, grid=(r // tm,), in_specs=in_specs, out_specs=c_out, out_shape=c_shape,
        compiler_params=_cparams("parallel"), name="ab_out",
    )(o_f, o_b, p, nl, onorm_g.reshape(1, HGRN_DK), w_out, *c_args)


def ssd_out(y_f, y_b, xa, p, dsk, ng, w_out, x, gate, g1, sh, sc, wrt, *, rows_per_mod, tm, xcol):
    tm = min(tm, rows_per_mod)
    r = x.shape[0]
    inner = w_out.shape[0]
    row = lambda i: (i, 0)
    c_in, c_out, c_shape, c_args = _out_common(x, gate, g1, sh, sc, wrt, rows_per_mod=rows_per_mod, tm=tm, xcol=xcol)
    in_specs = [pl.BlockSpec((tm, inner), row), pl.BlockSpec((tm, inner), row),
                pl.BlockSpec((tm, inner), row),
                pl.BlockSpec((tm, inner), row),
                pl.BlockSpec((1, inner), lambda i: (0, 0)), pl.BlockSpec((1, inner), lambda i: (0, 0)),
                pl.BlockSpec(w_out.shape, lambda i: (0, 0))] + c_in
    return pl.pallas_call(
        functools.partial(_ssd_out_kernel, group_width=inner // SSD_GROUPS),
        grid=(r // tm,), in_specs=in_specs, out_specs=c_out, out_shape=c_shape,
        compiler_params=_cparams("parallel"), name="ssd_out",
    )(y_f, y_b, xa, p, dsk.reshape(1, inner), ng.reshape(1, inner), w_out, *c_args)


F32_INF_BITS = 0x7F800000


def _lane_cumsum(src_ref, dst_ref, n_chunks):
    e = src_ref.shape[0]
    upper = _tri(LANES, True).astype(BF16)

    def body(j, off):
        ds = pl.ds(pl.multiple_of(j * LANES, LANES), LANES)
        c = _dot(src_ref[:, ds].astype(BF16), upper) + off
        dst_ref[:, ds] = c
        return c[:, LANES - 1:LANES]

    lax.fori_loop(0, n_chunks, body, jnp.zeros((e, 1), F32))


def _select_kernel(aff_ref, idx_ref, gate_ref, m_scr, c_scr, *, seq, cap, nph, kc):
    e = aff_ref.shape[1]
    n_chunks = seq // LANES
    bits = pltpu.bitcast(aff_ref[0], I32)

    def bisect(_, lohi):
        lo, hi = lohi
        mid = lo + lax.shift_right_logical(hi - lo, 1)
        cnt = jnp.sum((bits >= mid).astype(F32), axis=1, keepdims=True)
        ge = cnt >= cap
        return jnp.where(ge, mid, lo), jnp.where(ge, hi, mid)

    thr, _ = lax.fori_loop(0, 31, bisect, (jnp.zeros((e, 1), I32), jnp.full((e, 1), F32_INF_BITS, I32)))
    gt = bits > thr
    eq = bits == thr
    need = cap - jnp.sum(gt.astype(F32), axis=1, keepdims=True)
    m_scr[...] = eq.astype(F32)
    _lane_cumsum(m_scr, c_scr, n_chunks)
    sel = gt | (eq & (c_scr[...] <= need))
    m_scr[...] = sel.astype(F32)
    _lane_cumsum(m_scr, c_scr, n_chunks)

    sub_p = lax.broadcasted_iota(I32, (nph, kc), 0).astype(F32)
    sub_l = lax.broadcasted_iota(I32, (LANES, kc), 0).astype(F32)
    lane_t = lax.broadcasted_iota(I32, (1, kc), 1)
    inv = 1.0 / LANES
    for ex in range(e):
        def body(c, acc):
            ds = pl.ds(pl.multiple_of(c * kc, LANES), kc)
            pos = c_scr[ex:ex + 1, ds] - 1.0
            phi = jnp.floor(pos * inv)
            plo = pos - LANES * phi
            hit = (phi == sub_p) & (m_scr[ex:ex + 1, ds] > 0.0)
            t = (lane_t + c * kc).astype(F32)
            th = jnp.floor(t * inv)
            tl = t - LANES * th
            a1, a2, a3 = _split3(aff_ref[0, ex:ex + 1, ds])
            rows = [th, tl, a1.astype(F32), a2.astype(F32), a3.astype(F32)]
            lhs = jnp.concatenate([jnp.where(hit, r, 0.0) for r in rows], axis=0).astype(BF16)
            onehot = (plo == sub_l).astype(BF16)
            return acc + _dot_nt(lhs, onehot)

        acc = lax.fori_loop(0, seq // kc, body, jnp.zeros((5 * nph, LANES), F32))
        idx_ref[0, ex] = (acc[0:nph] * LANES + acc[nph:2 * nph]).astype(I32)
        gate_ref[0, ex] = acc[2 * nph:3 * nph] + acc[3 * nph:4 * nph] + acc[4 * nph:5 * nph]


def expert_select(aff_t, cap):
    bsz, e, seq = aff_t.shape
    nph = -(-cap // LANES)
    nph = -(-nph // SUBLANES) * SUBLANES
    kc = min(2048, seq)
    idx, gate = pl.pallas_call(
        functools.partial(_select_kernel, seq=seq, cap=cap, nph=nph, kc=kc),
        grid=(bsz,),
        in_specs=[pl.BlockSpec((1, e, seq), lambda b: (b, 0, 0))],
        out_specs=[pl.BlockSpec((1, e, nph, LANES), lambda b: (b, 0, 0, 0)),
                   pl.BlockSpec((1, e, nph, LANES), lambda b: (b, 0, 0, 0))],
        out_shape=[jax.ShapeDtypeStruct((bsz, e, nph, LANES), I32),
                   jax.ShapeDtypeStruct((bsz, e, nph, LANES), F32)],
        scratch_shapes=[pltpu.VMEM((e, seq), F32), pltpu.VMEM((e, seq), F32)],
        compiler_params=_cparams("parallel"),
        name="expert_select",
    )(aff_t)
    return idx.reshape(bsz, e, nph * LANES)[:, :, :cap], gate.reshape(bsz, e, nph * LANES)[:, :, :cap]


FFN_SLOTS = 3


def _ffn_kernel(idx_ref, gate_ref, g1_ref, sh_ref, sc_ref, g5_ref, w1_ref, w3_ref, w2_ref, xa_in,
                xa_hbm, buf, gsem, ssem, *, seq, tr, nt, d):
    del xa_in
    j = pl.program_id(2)
    base = pl.program_id(0) * seq
    step = (pl.program_id(0) * pl.num_programs(1) + pl.program_id(1)) * nt + j
    slot = step % FFN_SLOTS
    nslot = (step + 1) % FFN_SLOTS

    def start_gathers(tile, sl):
        def body(r, c):
            t = base + idx_ref[0, 0, tile * tr + r]
            pltpu.make_async_copy(xa_hbm.at[pl.ds(t, 1)], buf.at[sl, pl.ds(r, 1)], gsem.at[sl]).start()
            return c
        lax.fori_loop(0, tr, body, 0, unroll=min(8, tr))

    def start_scatters(tile, sl):
        def body(r, c):
            t = base + idx_ref[0, 0, tile * tr + r]
            pltpu.make_async_copy(buf.at[sl, pl.ds(r, 1), pl.ds(d, d)], xa_hbm.at[pl.ds(t, 1), pl.ds(d, d)],
                                  ssem.at[sl]).start()
            return c
        lax.fori_loop(0, tr, body, 0, unroll=min(8, tr))

    def wait_gathers(sl):
        pltpu.make_async_copy(xa_hbm.at[pl.ds(0, tr)], buf.at[sl], gsem.at[sl]).wait()

    def wait_scatters(sl):
        pltpu.make_async_copy(buf.at[sl, :, pl.ds(d, d)], xa_hbm.at[pl.ds(0, tr), pl.ds(d, d)], ssem.at[sl]).wait()

    @pl.when(j == 0)
    def _():
        start_gathers(0, slot)

    @pl.when(j + 1 < nt)
    def _():
        @pl.when(j >= FFN_SLOTS - 1)
        def _():
            wait_scatters(nslot)

        start_gathers(j + 1, nslot)

    wait_gathers(slot)
    f = _rms_mod(buf[slot, :, 0:d], g1_ref[...], sh_ref[0], sc_ref[0]).astype(BF16)
    hid = (_silu(_dot(f, w1_ref[0])) * _dot(f, w3_ref[0])).astype(BF16)
    y = _dot(hid, w2_ref[0])
    eye = lax.broadcasted_iota(I32, (tr, tr), 0) == lax.broadcasted_iota(I32, (tr, tr), 1)
    gcol = jnp.sum(jnp.where(eye, gate_ref[0], 0.0), axis=1, keepdims=True)
    buf[slot, :, d:2 * d] = buf[slot, :, d:2 * d] + (g5_ref[0] * gcol) * y
    start_scatters(j, slot)

    @pl.when(j == nt - 1)
    def _():
        for back in range(min(FFN_SLOTS, nt)):
            wait_scatters((step - back) % FFN_SLOTS)


def expert_ffn_combine(xa, idx, gate, g1, sh, sc, g5, w1, w3, w2, *, seq):
    bsz, e, cap = idx.shape
    d, ff = w1.shape[1], w1.shape[2]
    tr = min(256, cap)
    nt = cap // tr
    mod = lambda b, ex, j: (b, 0, 0)
    return pl.pallas_call(
        functools.partial(_ffn_kernel, seq=seq, tr=tr, nt=nt, d=d),
        grid=(bsz, e, nt),
        in_specs=[pl.BlockSpec((1, 1, cap), lambda b, ex, j: (b * e + ex, 0, 0), memory_space=pltpu.SMEM),
                  pl.BlockSpec((1, 1, tr), lambda b, ex, j: ((b * e + ex) * nt + j, 0, 0)),
                  pl.BlockSpec((1, d), lambda b, ex, j: (0, 0)),
                  pl.BlockSpec((1, 1, d), mod), pl.BlockSpec((1, 1, d), mod), pl.BlockSpec((1, 1, d), mod),
                  pl.BlockSpec((1, d, ff), lambda b, ex, j: (ex, 0, 0)),
                  pl.BlockSpec((1, d, ff), lambda b, ex, j: (ex, 0, 0)),
                  pl.BlockSpec((1, ff, d), lambda b, ex, j: (ex, 0, 0)),
                  pl.BlockSpec(memory_space=pl.ANY)],
        out_specs=pl.BlockSpec(memory_space=pl.ANY),
        out_shape=jax.ShapeDtypeStruct(xa.shape, F32),
        scratch_shapes=[pltpu.VMEM((FFN_SLOTS, tr, 2 * d), F32),
                        pltpu.SemaphoreType.DMA((FFN_SLOTS,)), pltpu.SemaphoreType.DMA((FFN_SLOTS,))],
        input_output_aliases={9: 0},
        compiler_params=_cparams("arbitrary", "arbitrary", "arbitrary"),
        name="expert_ffn_combine",
    )(idx.reshape(bsz * e, 1, cap), gate.reshape(bsz * e * nt, 1, tr), g1.reshape(1, d), sh, sc, g5,
      w1, w3, w2, xa)


CONV_HALO = 8


def _ssd_prep_kernel(cur_ref, prev_ref, next_ref, w_ref, b_ref, dt_ref, dtb_ref, xa_ref, dtv_ref, ext_ref,
                     *, tm, per):
    i = pl.program_id(0)
    first = (i % per) == 0
    last = (i % per) == per - 1
    ext_ref[0:CONV_HALO, :] = jnp.where(first, 0.0, prev_ref[...])
    ext_ref[CONV_HALO:CONV_HALO + tm, :] = cur_ref[...]
    ext_ref[CONV_HALO + tm:2 * CONV_HALO + tm, :] = jnp.where(last, 0.0, next_ref[...])
    acc = b_ref[...] + w_ref[0:1, :] * ext_ref[pl.ds(CONV_HALO - SSD_CONV // 2, tm), :]
    for k in range(1, SSD_CONV):
        acc = acc + w_ref[k:k + 1, :] * ext_ref[pl.ds(CONV_HALO - SSD_CONV // 2 + k, tm), :]
    xa_ref[...] = _silu(acc)
    v = dt_ref[...] + dtb_ref[...]
    dtv_ref[...] = jnp.maximum(v, 0.0) + jnp.log1p(jnp.exp(-jnp.abs(v)))


def ssd_prep(p, conv_w, conv_b, dt_bias, *, rows_per_seq, tm, col_xbc, col_dt):
    r = p.shape[0]
    cdim = conv_w.shape[1]
    half = cdim // 2
    tm = min(tm, rows_per_seq)
    per = rows_per_seq // tm
    hb = tm // CONV_HALO
    nhb = r // CONV_HALO
    c0 = col_xbc // half
    return pl.pallas_call(
        functools.partial(_ssd_prep_kernel, tm=tm, per=per),
        grid=(r // tm, 2),
        in_specs=[pl.BlockSpec((tm, half), lambda i, j: (i, c0 + j)),
                  pl.BlockSpec((CONV_HALO, half), lambda i, j: (jnp.maximum(i * hb - 1, 0), c0 + j)),
                  pl.BlockSpec((CONV_HALO, half), lambda i, j: (jnp.minimum((i + 1) * hb, nhb - 1), c0 + j)),
                  pl.BlockSpec((SSD_CONV, half), lambda i, j: (0, j)),
                  pl.BlockSpec((1, half), lambda i, j: (0, j)),
                  pl.BlockSpec((tm, LANES), lambda i, j: (i, col_dt // LANES)),
                  pl.BlockSpec((1, LANES), lambda i, j: (0, 0))],
        out_specs=[pl.BlockSpec((tm, half), lambda i, j: (i, j)),
                   pl.BlockSpec((tm, LANES), lambda i, j: (i, 0))],
        out_shape=[jax.ShapeDtypeStruct((r, cdim), F32), jax.ShapeDtypeStruct((r, LANES), F32)],
        scratch_shapes=[pltpu.VMEM((tm + 2 * CONV_HALO, half), F32)],
        compiler_params=_cparams("parallel", "arbitrary"),
        name="ssd_prep",
    )(p, p, p, conv_w, conv_b.reshape(1, cdim), p, dt_bias)


SSD_HPG = 4


def _ssd_block(x, bm, cm, dtv, a_row, hs, lane0, upper, need_y=True):
    t = x.shape[0]
    hp = x.shape[1]
    pdim = hp // SSD_HPG
    ch = SSD_CHUNK
    nc = t // ch
    expand = (lax.broadcasted_iota(I32, (LANES, hp), 0) ==
              lane0 + lax.broadcasted_iota(I32, (LANES, hp), 1) // pdim).astype(BF16)
    dtx = sum(_dot(p, expand) for p in _split3(dtv))
    a_x = sum(_dot(p, expand) for p in _split3(jnp.broadcast_to(a_row, (SUBLANES, LANES))))[0:1]
    dtax = dtx * a_x
    c3 = _dot(_block_tri(t, ch, upper).astype(BF16), jnp.concatenate(_split3(dtax), axis=1))
    acx = c3[:, 0:hp] + c3[:, hp:2 * hp] + c3[:, 2 * hp:3 * hp]
    r = lax.broadcasted_iota(I32, (ch, hp), 0)
    s = lax.broadcasted_iota(I32, (ch, hp), 1) % ch
    src_le = (s <= r) if upper else (s >= r)
    ident = s == r
    causal = (r <= s) if upper else (r >= s)
    same_head = (lax.broadcasted_iota(I32, (hp, hp), 0) // pdim) == (lax.broadcasted_iota(I32, (hp, hp), 1) // pdim)
    ydiag, decay_out, alasts, upds = [], [], [], []
    for c in range(nc):
        rs = slice(c * ch, (c + 1) * ch)
        acx_c, dtx_c, x_c = acx[rs], dtx[rs], x[rs]
        bm_c = bm[rs].astype(BF16)
        alast = acx_c[0:1] if upper else acx_c[ch - 1:ch]
        if need_y:
            arx = jnp.sum(jnp.where(src_le, dtax[rs], 0.0), axis=0, keepdims=True)
            dtr = jnp.sum(jnp.where(ident, dtx_c, 0.0), axis=0, keepdims=True)
            dec = jnp.where(causal, jnp.exp(jnp.minimum(acx_c - arx, 0.0)), 0.0)
            cbx = _dot_nt(cm[rs].astype(BF16), jnp.concatenate([bm_c] * SSD_HPG, axis=0))
            gx = (cbx * dec * dtr).astype(BF16)
            xbd = jnp.where(same_head, jnp.concatenate([x_c] * SSD_HPG, axis=0), 0.0).astype(BF16)
            ydiag.append(_dot(gx, xbd))
            decay_out.append(jnp.exp(acx_c))
        alasts.append(alast)
        upds.append(_dot_tn(bm_c, (x_c * (jnp.exp(alast - acx_c) * dtx_c)).astype(BF16)))
    ys = [None] * nc
    for c in (reversed(range(nc)) if upper else range(nc)):
        if need_y:
            ys[c] = ydiag[c] + _dot(cm[c * ch:(c + 1) * ch].astype(BF16), hs.astype(BF16)) * decay_out[c]
        hs = hs * jnp.exp(alasts[c]) + upds[c]
    return (jnp.concatenate(ys, axis=0) if need_y else None), hs


def _ssd_kernel(xf_ref, bf_ref, cf_ref, df_ref, xb_ref, bb_ref, cb_ref, db_ref,
                xc_ref, bc_ref, cc_ref, dc_ref, alog_ref, yf_ref, yb_ref, hf_ref, hb_ref):
    g = pl.program_id(0) % SSD_GROUPS
    a_row = -jnp.exp(alog_ref[...])
    lane_f = g * SSD_HPG
    lane_b = SSD_GROUPS * SSD_HPG + g * SSD_HPG

    @pl.when(pl.program_id(1) == 0)
    def _():
        z0 = jnp.zeros(hf_ref.shape, F32)
        ctx = (xc_ref[...], bc_ref[...], cc_ref[...], dc_ref[...], a_row, z0)
        _, hf_ref[...] = _ssd_block(*ctx, lane_f, False, need_y=False)
        _, hb_ref[...] = _ssd_block(*ctx, lane_b, True, need_y=False)

    yf_ref[...], hf_ref[...] = _ssd_block(xf_ref[...], bf_ref[...], cf_ref[...], df_ref[...], a_row, hf_ref[...],
                                          lane_f, False)
    yb_ref[...], hb_ref[...] = _ssd_block(xb_ref[...], bb_ref[...], cb_ref[...], db_ref[...], a_row, hb_ref[...],
                                          lane_b, True)


def ssd_scan(xa, dtv, xa_c, dtv_c, a_log_row, *, bsz, seq, lc, tb, inner):
    g = SSD_GROUPS
    n = SSD_STATE
    xw = inner // g
    nb = seq // tb
    cb0 = inner // n
    cc0 = cb0 + g

    def blk(i, rev):
        return nb - 1 - i if rev else i

    def lat(width, col, rev):
        return pl.BlockSpec((tb, width), lambda bg, i: ((bg // g) * nb + blk(i, rev), col + bg % g))

    def lat_dt(rev):
        return pl.BlockSpec((tb, LANES), lambda bg, i: ((bg // g) * nb + blk(i, rev), 0))

    def ctx(width, col):
        return pl.BlockSpec((lc, width), lambda bg, i: (bg // g, col + bg % g))

    def out(rev):
        return pl.BlockSpec((tb, xw), lambda bg, i: ((bg // g) * nb + blk(i, rev), bg % g))

    return pl.pallas_call(
        _ssd_kernel,
        grid=(bsz * g, nb),
        in_specs=[lat(xw, 0, False), lat(n, cb0, False), lat(n, cc0, False), lat_dt(False),
                  lat(xw, 0, True), lat(n, cb0, True), lat(n, cc0, True), lat_dt(True),
                  ctx(xw, 0), ctx(n, cb0), ctx(n, cc0), pl.BlockSpec((lc, LANES), lambda bg, i: (bg // g, 0)),
                  pl.BlockSpec((1, LANES), lambda bg, i: (0, 0))],
        out_specs=[out(False), out(True)],
        out_shape=[jax.ShapeDtypeStruct((bsz * seq, inner), F32), jax.ShapeDtypeStruct((bsz * seq, inner), F32)],
        scratch_shapes=[pltpu.VMEM((n, xw), F32), pltpu.VMEM((n, xw), F32)],
        compiler_params=_cparams("parallel", "arbitrary"),
        name="ssd_scan",
    )(xa, xa, xa, dtv, xa, xa, xa, dtv, xa_c, xa_c, xa_c, dtv_c, a_log_row)


def _final_norm_kernel(x_ref, g_ref, o_ref):
    x = x_ref[...]
    o_ref[...] = x * lax.rsqrt(jnp.mean(x * x, axis=-1, keepdims=True) + RMS_EPS) * g_ref[...]


def final_norm(x, g, *, tm, xcol):
    r = x.shape[0]
    d = g.shape[0]
    return pl.pallas_call(
        _final_norm_kernel,
        grid=(r // tm,),
        in_specs=[pl.BlockSpec((tm, d), lambda i: (i, xcol)), pl.BlockSpec((1, d), lambda i: (0, 0))],
        out_specs=pl.BlockSpec((tm, d), lambda i: (i, 0)),
        out_shape=jax.ShapeDtypeStruct((r, d), F32),
        compiler_params=_cparams("parallel"),
        name="final_norm",
    )(x, g.reshape(1, d))


def _moe(xa, aff, g1, sh, sc, g5, w1, w3, w2, *, seq):
    cap = max(1, CAPACITY_FACTOR * seq // N_EXPERTS)
    idx, gate = expert_select(aff, cap)
    return expert_ffn_combine(xa, idx, gate, g1, sh, sc, g5, w1, w3, w2, seq=seq)


def kernel(x, c, ctx, c_ctx, ada_w, ada_b, norm_g, final_g, ab_w_in, ab_w_out, hgrn_lb_logits, hgrn_onorm_g, na_rpb, ssd_w_in, ssd_conv_w, ssd_conv_b, ssd_a_log, ssd_dt_bias, ssd_d, ssd_norm_g, ssd_w_out, moe_router, moe_w1, moe_w3, moe_w2):
    bsz, seq, d = x.shape
    lc = ctx.shape[1]
    depth = ada_w.shape[0]
    assert depth == 2, "context outputs of an SSD layer are not implemented (only needed when a layer follows it)"
    mods = modulation_table(c, c_ctx, ada_w, ada_b)
    lb_all = jnp.cumsum(jax.nn.softmax(hgrn_lb_logits.astype(F32), axis=1), axis=1)
    xl = x.reshape(bsz * seq, d)
    xc = ctx.reshape(bsz * lc, d)
    xcol = 0
    hg = 5 * HGRN_WIDTH
    for l in range(depth):
        need_ctx = l < depth - 1
        k = l // 2
        m = mods[l]
        ml = [m[:bsz, j * d:(j + 1) * d].reshape(bsz, 1, d) for j in range(6)]
        mc = [jnp.broadcast_to(m[bsz, j * d:(j + 1) * d].reshape(1, 1, d), (bsz, 1, d)) for j in range(6)]
        wrt = moe_router[l].T
        w1, w3, w2 = (w[l].astype(BF16) for w in (moe_w1, moe_w3, moe_w2))
        g0, g1 = norm_g[l, 0], norm_g[l, 1]
        if l % 2 == 0:
            w_in = ab_w_in[k].astype(BF16)
            w_out = ab_w_out[k].astype(BF16)
            p_lat = norm_mod_project(xl, g0, ml[0], ml[1], w_in, rows_per_mod=seq, tm=512, tn=1024, xcol=xcol)
            p_ctx = norm_mod_project(xc, g0, mc[0], mc[1], w_in, rows_per_mod=lc, tm=512, tn=1024, xcol=xcol)
            o_f, o_b, oc_f, oc_b = hgrn_scan(p_lat, p_ctx, lb_all[:, k], bsz=bsz, seq=seq, lc=lc, tb=min(256, seq))
            nl = neighborhood_attention(p_lat, p_ctx, na_bias_table(na_rpb[k]), bsz=bsz, seq=seq, lc=lc, col0=hg)
            xa, aff = ab_out(o_f, o_b, p_lat, nl, hgrn_onorm_g[k], w_out, xl, ml[2], g1, ml[3], ml[4], wrt,
                             rows_per_mod=seq, tm=512, xcol=xcol)
            if need_ctx:
                nc = context_attention(p_ctx, bsz=bsz, lc=lc, col0=hg)
                ca, caff = ab_out(oc_f, oc_b, p_ctx, nc, hgrn_onorm_g[k], w_out, xc, mc[2], g1, mc[3], mc[4],
                                  wrt, rows_per_mod=lc, tm=512, xcol=xcol)
        else:
            inner = ssd_w_out.shape[1]
            cdim = ssd_conv_w.shape[2]
            heads2 = ssd_dt_bias.shape[1] * ssd_dt_bias.shape[2]
            wz = ssd_w_in[k]
            pad = jnp.zeros((d, LANES - heads2), F32)
            w_in = jnp.concatenate([wz, pad], axis=1).astype(BF16)
            w_out = ssd_w_out[k].astype(BF16)
            tn = w_in.shape[1] // 7
            p_lat = norm_mod_project(xl, g0, ml[0], ml[1], w_in, rows_per_mod=seq, tm=256, tn=tn, xcol=xcol)
            p_ctx = norm_mod_project(xc, g0, mc[0], mc[1], w_in, rows_per_mod=lc, tm=256, tn=tn, xcol=xcol)
            dtb = jnp.zeros((1, LANES), F32).at[0, :heads2].set(ssd_dt_bias[k].reshape(-1))
            alog = jnp.zeros((1, LANES), F32).at[0, :heads2].set(ssd_a_log[k].reshape(-1))
            prep = functools.partial(ssd_prep, conv_w=ssd_conv_w[k], conv_b=ssd_conv_b[k], dt_bias=dtb, tm=256,
                                     col_xbc=inner, col_dt=inner + cdim)
            xs, dtv = prep(p_lat, rows_per_seq=seq)
            xs_c, dtv_c = prep(p_ctx, rows_per_seq=lc)
            y_f, y_b = ssd_scan(xs, dtv, xs_c, dtv_c, alog, bsz=bsz, seq=seq, lc=lc, tb=min(256, seq), inner=inner)
            dsk = jnp.repeat(ssd_d[k].astype(F32), SSD_HEADDIM)
            xa, aff = ssd_out(y_f, y_b, xs, p_lat, dsk, ssd_norm_g[k], w_out, xl, ml[2], g1, ml[3], ml[4], wrt,
                              rows_per_mod=seq, tm=256, xcol=xcol)
        xl = _moe(xa, aff, g1, ml[3], ml[4], ml[5], w1, w3, w2, seq=seq)
        if need_ctx:
            xc = _moe(ca, caff, g1, mc[3], mc[4], mc[5], w1, w3, w2, seq=lc)
        xcol = 1
    return final_norm(xl, final_g, tm=512, xcol=xcol).reshape(bsz, seq, d)
```

```python
import functools
import math

import jax
import jax.numpy as jnp
import numpy as np
from jax import lax
from jax.experimental import pallas as pl
from jax.experimental.pallas import tpu as pltpu

F32 = jnp.float32
BF16 = jnp.bfloat16
I32 = jnp.int32

GRID_W = 64
HGRN_HEADS = 4
HGRN_DK = 128
HGRN_WIDTH = HGRN_HEADS * HGRN_DK
GLA_CHUNK = 32
NA_HEADS = 8
NA_DH = 64
NA_WIDTH = NA_HEADS * NA_DH
NA_KR = 8
NA_KC = 16
SSD_HEADDIM = 64
SSD_GROUPS = 8
SSD_STATE = 128
SSD_CONV = 5
SSD_CHUNK = 64
N_EXPERTS = 16
CAPACITY_FACTOR = 2
RMS_EPS = 1e-6
NEG_INF = -1e30

LANES = 128
SUBLANES = 8
VMEM_LIMIT_BYTES = 56 * 1024 * 1024


def _cparams(*sem, flags=None):
    return pltpu.CompilerParams(dimension_semantics=sem, vmem_limit_bytes=VMEM_LIMIT_BYTES, flags=flags)


SCAN_FLAGS = None


def _sigmoid(x):
    return 1.0 / (1.0 + jnp.exp(-x))


def _silu(x):
    return x * _sigmoid(x)


def _split3(x):
    a = x.astype(BF16)
    r = x - a.astype(F32)
    b = r.astype(BF16)
    c = (r - b.astype(F32)).astype(BF16)
    return a, b, c


def _dot(a, b):
    return jnp.dot(a, b, preferred_element_type=F32)


def _dot_nt(a, b):
    return lax.dot_general(a, b, (((1,), (1,)), ((), ())), preferred_element_type=F32)


def _dot_tn(a, b):
    return lax.dot_general(a, b, (((0,), (0,)), ((), ())), preferred_element_type=F32)


def _dot_exact_lhs(m01, x):
    m = m01.astype(BF16)
    a, b, c = _split3(x)
    return _dot(m, a) + _dot(m, b) + _dot(m, c)


def _dot3(a, b):
    a1 = a.astype(BF16)
    a2 = (a - a1.astype(F32)).astype(BF16)
    b1 = b.astype(BF16)
    b2 = (b - b1.astype(F32)).astype(BF16)
    return _dot(a1, b1) + _dot(a1, b2) + _dot(a2, b1)


def _dot3_nt(a, b):
    a1 = a.astype(BF16)
    a2 = (a - a1.astype(F32)).astype(BF16)
    b1 = b.astype(BF16)
    b2 = (b - b1.astype(F32)).astype(BF16)
    return _dot_nt(a1, b1) + _dot_nt(a1, b2) + _dot_nt(a2, b1)


def _rms_mod(x, g, shift, scale):
    y = x * lax.rsqrt(jnp.mean(x * x, axis=-1, keepdims=True) + RMS_EPS) * g
    return y * (1.0 + scale) + shift


def _row_pitch(d):
    rows = 2 * (d // LANES)
    return rows + (4 - rows % SUBLANES) % SUBLANES


def _load_packed(ref, n, part, d):
    nx = d // LANES
    pitch = _row_pitch(d)
    return jnp.concatenate([ref[pl.ds(part * nx + k, n, stride=pitch), :] for k in range(nx)], axis=1)


def _store_packed(ref, val, n, part, d):
    nx = d // LANES
    pitch = _row_pitch(d)
    for k in range(nx):
        ref[pl.ds(part * nx + k, n, stride=pitch), :] = val[:, k * LANES:(k + 1) * LANES]


def _load_stream(x_ref, packed):
    if not packed:
        return x_ref[...]
    n, d = packed
    return _load_packed(x_ref, n, 1, d)


def _stream_spec(tm, d, packed, index_map_row):
    if packed:
        return pl.BlockSpec((tm * _row_pitch(d), LANES), lambda *a: (index_map_row(*a), 0))
    return pl.BlockSpec((tm, d), lambda *a: (index_map_row(*a), 0))


def _mod_kernel(s_ref, w_ref, b_ref, o_ref):
    s = _silu(s_ref[...])
    o_ref[0] = _dot3(s, w_ref[0]) + b_ref[0]


def modulation_table(c, c_ctx, ada_w, ada_b):
    depth, d, n = ada_w.shape
    bsz = c.shape[0]
    s = jnp.zeros((SUBLANES, d), F32).at[:bsz].set(c).at[bsz].set(c_ctx)
    tn = 1024
    return pl.pallas_call(
        _mod_kernel,
        grid=(depth, n // tn),
        in_specs=[pl.BlockSpec((SUBLANES, d), lambda l, j: (0, 0)),
                  pl.BlockSpec((1, d, tn), lambda l, j: (l, 0, j)),
                  pl.BlockSpec((1, 1, tn), lambda l, j: (l, 0, j))],
        out_specs=pl.BlockSpec((1, SUBLANES, tn), lambda l, j: (l, 0, j)),
        out_shape=jax.ShapeDtypeStruct((depth, SUBLANES, n), F32),
        compiler_params=_cparams("parallel", "parallel"),
        name="mod_table",
    )(s, ada_w, ada_b.reshape(depth, 1, n))


def _proj_kernel(x_ref, g_ref, sh_ref, sc_ref, w_ref, o_ref, *, tn, packed):
    h = _rms_mod(_load_stream(x_ref, packed), g_ref[...], sh_ref[0], sc_ref[0]).astype(BF16)
    for j in range(w_ref.shape[1] // tn):
        o_ref[:, j * tn:(j + 1) * tn] = _dot(h, w_ref[:, j * tn:(j + 1) * tn])


def norm_mod_project(x, g, shift, scale, w, *, rows, rows_per_mod, tm, tn, packed):
    r = rows
    d, n = w.shape
    tm = min(tm, rows_per_mod)
    per = rows_per_mod // tm
    return pl.pallas_call(
        functools.partial(_proj_kernel, tn=tn, packed=(tm, d) if packed else None),
        grid=(r // tm,),
        in_specs=[_stream_spec(tm, d, packed, lambda i: i),
                  pl.BlockSpec((1, d), lambda i: (0, 0)),
                  pl.BlockSpec((1, 1, d), lambda i: (i // per, 0, 0)),
                  pl.BlockSpec((1, 1, d), lambda i: (i // per, 0, 0)),
                  pl.BlockSpec((d, n), lambda i: (0, 0))],
        out_specs=pl.BlockSpec((tm, n), lambda i: (i, 0)),
        out_shape=jax.ShapeDtypeStruct((r, n), F32),
        compiler_params=_cparams("parallel"),
        name="norm_mod_project",
    )(x, g.reshape(1, d), shift, scale, w)


def _tri(n, upper):
    r = lax.broadcasted_iota(I32, (n, n), 0)
    c = lax.broadcasted_iota(I32, (n, n), 1)
    return (r <= c) if upper else (r >= c)


SCAN_SUB = 256


def _block_tri(n, chunk, upper):
    r = lax.broadcasted_iota(I32, (n, n), 0)
    c = lax.broadcasted_iota(I32, (n, n), 1)
    same = (r // chunk) == (c // chunk)
    return same & ((r <= c) if upper else (r >= c))


def _hgrn_block(q, z, v, lb, st, upper):
    n = q.shape[0]
    ch = GLA_CHUNK
    nc = n // ch
    mask = _block_tri(n, ch, upper)
    qs = _silu(q)
    f = lb + (1.0 - lb) * _sigmoid(z)
    g = jnp.log(f)
    k = 1.0 - f
    c3 = _dot(mask.astype(BF16), jnp.concatenate(_split3(g), axis=1))
    dk = g.shape[1]
    cum = c3[:, 0:dk] + c3[:, dk:2 * dk] + c3[:, 2 * dk:3 * dk]
    cls = [cum[c * ch:c * ch + 1] if upper else cum[(c + 1) * ch - 1:(c + 1) * ch] for c in range(nc)]
    clb = jnp.concatenate([jnp.broadcast_to(cl, (ch, dk)) for cl in cls], axis=0)
    q_in = (qs * jnp.exp(cum)).astype(BF16)
    k_in = (k * jnp.exp(-cum)).astype(BF16)
    k_out = (k * jnp.exp(clb - cum)).astype(BF16)
    vb = v.astype(BF16)
    attn = jnp.where(mask, _dot_nt(q_in, k_in), 0.0)
    intra = _dot(attn.astype(BF16), vb)
    dv = v.shape[1]
    vt = v.T
    chunk_of_col = lax.broadcasted_iota(I32, (1, n), 1) // ch
    vexp_t = jnp.concatenate([jnp.where(chunk_of_col == c, vt, 0.0) for c in range(nc)], axis=0).astype(BF16)
    upd = _dot(vexp_t, k_out)
    sts = [None] * nc
    for c in (reversed(range(nc)) if upper else range(nc)):
        sts[c] = st
        st = st * jnp.exp(cls[c]) + upd[c * dv:(c + 1) * dv]
    chunk_of_row = lax.broadcasted_iota(I32, (n, 1), 0) // ch
    q_exp = jnp.concatenate([jnp.where(chunk_of_row == c, q_in, jnp.zeros_like(q_in)) for c in range(nc)], axis=1)
    st_cat = jnp.concatenate(sts, axis=1).astype(BF16)
    return intra + _dot_nt(q_exp, st_cat), st


def _hgrn_kernel(qf_ref, zf_ref, vf_ref, qb_ref, zb_ref, vb_ref,
                 qc_ref, zfc_ref, zbc_ref, vc_ref, lb_ref,
                 of_ref, ob_ref, ocf_ref, ocb_ref, stf_ref, stb_ref):
    lbf = lb_ref[0:1]
    lbb = lb_ref[1:2]

    def run(q_ref, z_ref, v_ref, lb, o_ref, st, upper):
        n = q_ref.shape[0]
        sub = min(SCAN_SUB, n)
        order = range(n // sub)
        for sb in (reversed(order) if upper else order):
            rs = slice(sb * sub, (sb + 1) * sub)
            o_ref[rs, :], st = _hgrn_block(q_ref[rs, :], z_ref[rs, :], v_ref[rs, :], lb, st, upper)
        return st

    @pl.when(pl.program_id(1) == 0)
    def _():
        z0 = jnp.zeros((HGRN_DK, HGRN_DK), F32)
        stf_ref[...] = run(qc_ref, zfc_ref, vc_ref, lbf, ocf_ref, z0, False)
        stb_ref[...] = run(qc_ref, zbc_ref, vc_ref, lbb, ocb_ref, z0, True)

    stf_ref[...] = run(qf_ref, zf_ref, vf_ref, lbf, of_ref, stf_ref[...], False)
    stb_ref[...] = run(qb_ref, zb_ref, vb_ref, lbb, ob_ref, stb_ref[...], True)


def hgrn_scan(p_lat, p_ctx, lb, *, bsz, seq, lc, tb):
    h = HGRN_HEADS
    dk = HGRN_DK
    nb = seq // tb
    cq, czf, czb, cv = 0, h, 2 * h, 3 * h

    def lat(col, rev):
        if rev:
            return pl.BlockSpec((tb, dk), lambda bh, i: ((bh // h) * nb + nb - 1 - i, col + bh % h))
        return pl.BlockSpec((tb, dk), lambda bh, i: ((bh // h) * nb + i, col + bh % h))

    def ctx(col):
        return pl.BlockSpec((lc, dk), lambda bh, i: (bh // h, col + bh % h))

    def out_lat(rev):
        if rev:
            return pl.BlockSpec((tb, dk), lambda bh, i: ((bh // h) * nb + nb - 1 - i, bh % h))
        return pl.BlockSpec((tb, dk), lambda bh, i: ((bh // h) * nb + i, bh % h))

    out_ctx = pl.BlockSpec((lc, dk), lambda bh, i: (bh // h, bh % h))
    w = h * dk
    return pl.pallas_call(
        _hgrn_kernel,
        grid=(bsz * h, nb),
        in_specs=[lat(cq, False), lat(czf, False), lat(cv, False),
                  lat(cq, True), lat(czb, True), lat(cv, True),
                  ctx(cq), ctx(czf), ctx(czb), ctx(cv),
                  pl.BlockSpec((2, dk), lambda bh, i: (0, bh % h))],
        out_specs=[out_lat(False), out_lat(True), out_ctx, out_ctx],
        out_shape=[jax.ShapeDtypeStruct((bsz * seq, w), F32), jax.ShapeDtypeStruct((bsz * seq, w), F32),
                   jax.ShapeDtypeStruct((bsz * lc, w), F32), jax.ShapeDtypeStruct((bsz * lc, w), F32)],
        scratch_shapes=[pltpu.VMEM((dk, dk), F32), pltpu.VMEM((dk, dk), F32)],
        compiler_params=_cparams("parallel", "arbitrary", flags=SCAN_FLAGS),
        name="hgrn_scan",
    )(p_lat, p_lat, p_lat, p_lat, p_lat, p_lat, p_ctx, p_ctx, p_ctx, p_ctx, lb)


NA_RB = 8


def na_bias_table(rpb):
    w = GRID_W
    qc = np.arange(w)
    win0 = np.clip(qc - NA_KC // 2, 0, w - NA_KC)
    kc = np.arange(w)
    ok = (kc[None, :] >= win0[:, None]) & (kc[None, :] < win0[:, None] + NA_KC)
    dc = np.clip(kc[None, :] - qc[:, None] + NA_KC - 1, 0, 2 * NA_KC - 2)
    v = np.arange(NA_KR)
    j = np.arange(NA_KR)
    dr = j[None, :] - v[:, None] + NA_KR - 1
    t = rpb.astype(F32)[:, dr][:, :, :, dc]
    t = jnp.where(ok[None, None, None], t, NEG_INF)
    t = jnp.transpose(t, (0, 1, 3, 2, 4))
    return t.reshape(rpb.shape[0], NA_KR, w, NA_KR * w)


def _na_kernel(q_ref, kp_ref, kc_ref, kn_ref, vp_ref, vc_ref, vn_ref, kx_ref, vx_ref, bias_ref,
               o_ref, ks_ref, vs_ref, *, rows):
    w = GRID_W
    blk = NA_RB * w
    i = pl.program_id(2)
    ks_ref[0:blk, :] = kp_ref[...].astype(BF16)
    ks_ref[blk:2 * blk, :] = kc_ref[...].astype(BF16)
    ks_ref[2 * blk:3 * blk, :] = kn_ref[...].astype(BF16)
    vs_ref[0:blk, :] = vp_ref[...].astype(BF16)
    vs_ref[blk:2 * blk, :] = vc_ref[...].astype(BF16)
    vs_ref[2 * blk:3 * blk, :] = vn_ref[...].astype(BF16)
    kx = kx_ref[...].astype(BF16)
    vx = vx_ref[...].astype(BF16)
    scale = NA_DH ** -0.5
    for rr in range(NA_RB):
        r = i * NA_RB + rr
        r0 = jnp.clip(r - NA_KR // 2, 0, rows - NA_KR)
        start = pl.multiple_of((r0 - (i - 1) * NA_RB) * w, w)
        vv = r - r0
        outs = []
        for hh in range(2):
            cs = slice(hh * NA_DH, (hh + 1) * NA_DH)
            q = (q_ref[rr * w:(rr + 1) * w, cs] * scale).astype(BF16)
            kw = ks_ref[pl.ds(start, NA_KR * w), cs]
            vw = vs_ref[pl.ds(start, NA_KR * w), cs]
            s = _dot_nt(q, kw) + bias_ref[hh, vv]
            sx = _dot_nt(q, kx[:, cs])
            m = jnp.maximum(jnp.max(s, axis=-1, keepdims=True), jnp.max(sx, axis=-1, keepdims=True))
            p = jnp.exp(s - m)
            px = jnp.exp(sx - m)
            den = jnp.sum(p, axis=-1, keepdims=True) + jnp.sum(px, axis=-1, keepdims=True)
            o = _dot(p.astype(BF16), vw) + _dot(px.astype(BF16), vx[:, cs])
            outs.append(o / den)
        o_ref[rr * w:(rr + 1) * w, :] = jnp.concatenate(outs, axis=-1).astype(o_ref.dtype)


def neighborhood_attention(p_lat, p_ctx, bias_tbl, *, bsz, seq, lc, col0):
    w = GRID_W
    rows = seq // w
    blk = NA_RB * w
    nb = rows // NA_RB
    hp = NA_HEADS // 2
    pw = 2 * NA_DH
    cq = col0 // pw
    ck = cq + hp
    cv = ck + hp

    def lat(col, off):
        def im(b, h, i):
            return (b * nb + jnp.clip(i + off, 0, nb - 1), col + h)
        return pl.BlockSpec((blk, pw), im)

    def ctx(col):
        return pl.BlockSpec((lc, pw), lambda b, h, i: (b, col + h))

    return pl.pallas_call(
        functools.partial(_na_kernel, rows=rows),
        grid=(bsz, hp, nb),
        in_specs=[lat(cq, 0), lat(ck, -1), lat(ck, 0), lat(ck, 1), lat(cv, -1), lat(cv, 0), lat(cv, 1),
                  ctx(ck), ctx(cv),
                  pl.BlockSpec((2, NA_KR, w, NA_KR * w), lambda b, h, i: (h, 0, 0, 0))],
        out_specs=pl.BlockSpec((blk, pw), lambda b, h, i: (b * nb + i, h)),
        out_shape=jax.ShapeDtypeStruct((bsz * seq, NA_WIDTH), BF16),
        scratch_shapes=[pltpu.VMEM((3 * blk, pw), BF16), pltpu.VMEM((3 * blk, pw), BF16)],
        compiler_params=_cparams("parallel", "parallel", "arbitrary"),
        name="neighborhood_attention",
    )(p_lat, p_lat, p_lat, p_lat, p_lat, p_lat, p_lat, p_ctx, p_ctx, bias_tbl)


def _ctx_attn_kernel(q_ref, k_ref, v_ref, o_ref):
    scale = NA_DH ** -0.5
    outs = []
    for hh in range(2):
        cs = slice(hh * NA_DH, (hh + 1) * NA_DH)
        q = (q_ref[:, cs] * scale).astype(BF16)
        s = _dot_nt(q, k_ref[:, cs].astype(BF16))
        p = jnp.exp(s - jnp.max(s, axis=-1, keepdims=True))
        o = _dot(p.astype(BF16), v_ref[:, cs].astype(BF16))
        outs.append(o / jnp.sum(p, axis=-1, keepdims=True))
    o_ref[...] = jnp.concatenate(outs, axis=-1).astype(o_ref.dtype)


def context_attention(p_ctx, *, bsz, lc, col0):
    hp = NA_HEADS // 2
    pw = 2 * NA_DH
    cq = col0 // pw

    def spec(col):
        return pl.BlockSpec((lc, pw), lambda b, h: (b, col + h))

    return pl.pallas_call(
        _ctx_attn_kernel,
        grid=(bsz, hp),
        in_specs=[spec(cq), spec(cq + hp), spec(cq + 2 * hp)],
        out_specs=pl.BlockSpec((lc, pw), lambda b, h: (b, h)),
        out_shape=jax.ShapeDtypeStruct((bsz * lc, NA_WIDTH), BF16),
        compiler_params=_cparams("parallel", "parallel"),
        name="context_attention",
    )(p_ctx, p_ctx, p_ctx)


def _residual_router(x, y, gate, g1, sh, sc, wrt, xa_ref, aff_ref):
    x1 = x + gate * y
    n, d = x1.shape
    _store_packed(xa_ref, x1, n, 0, d)
    _store_packed(xa_ref, x1, n, 1, d)
    pitch = _row_pitch(d)
    for k in range(2 * (d // LANES), pitch):
        xa_ref[pl.ds(k, n, stride=pitch), :] = jnp.zeros((n, LANES), F32)
    f = _rms_mod(x1, g1, sh, sc)
    lt = _dot3_nt(wrt, f)
    e = jnp.exp(lt - jnp.max(lt, axis=0, keepdims=True))
    aff_ref[0] = e / jnp.sum(e, axis=0, keepdims=True)


def _ab_out_kernel(of_ref, ob_ref, gg_ref, nl_ref, on_ref, w_ref, x_ref, gate_ref, g1_ref, sh_ref, sc_ref, wrt_ref,
                   xa_ref, aff_ref, *, packed):
    o = of_ref[...] + ob_ref[...]
    gain = on_ref[...]
    parts = []
    for h in range(HGRN_HEADS):
        oh = o[:, h * HGRN_DK:(h + 1) * HGRN_DK]
        parts.append(oh * lax.rsqrt(jnp.mean(oh * oh, axis=-1, keepdims=True) + RMS_EPS) * gain)
    hl = (jnp.concatenate(parts, axis=-1) * _silu(gg_ref[...])).astype(BF16)
    y = _dot(hl, w_ref[0:HGRN_WIDTH, :]) + _dot(nl_ref[...], w_ref[HGRN_WIDTH:, :])
    _residual_router(_load_stream(x_ref, packed), y, gate_ref[0], g1_ref[...], sh_ref[0], sc_ref[0], wrt_ref[...],
                     xa_ref, aff_ref)


def _ssd_out_kernel(yf_ref, yb_ref, xs_ref, z_ref, dsk_ref, ng_ref, w_ref, x_ref, gate_ref, g1_ref, sh_ref, sc_ref,
                    wrt_ref, xa_ref, aff_ref, *, group_width, packed):
    y = (yf_ref[...] + yb_ref[...] + dsk_ref[...] * xs_ref[...]) * _silu(z_ref[...])
    parts = []
    for g in range(y.shape[1] // group_width):
        yg = y[:, g * group_width:(g + 1) * group_width]
        parts.append(yg * lax.rsqrt(jnp.mean(yg * yg, axis=-1, keepdims=True) + RMS_EPS))
    yn = (jnp.concatenate(parts, axis=-1) * ng_ref[...]).astype(BF16)
    _residual_router(_load_stream(x_ref, packed), _dot(yn, w_ref[...]), gate_ref[0], g1_ref[...], sh_ref[0],
                     sc_ref[0], wrt_ref[...], xa_ref, aff_ref)


def _out_common(x, gate, g1, sh, sc, wrt, *, rows, rows_per_mod, tm, packed):
    r = rows
    e, d = wrt.shape
    per = rows_per_mod // tm
    nbatch = r // rows_per_mod
    pitch = _row_pitch(d)
    mod = lambda i: (i // per, 0, 0)
    in_specs = [_stream_spec(tm, d, packed, lambda i: i), pl.BlockSpec((1, 1, d), mod),
                pl.BlockSpec((1, d), lambda i: (0, 0)),
                pl.BlockSpec((1, 1, d), mod), pl.BlockSpec((1, 1, d), mod), pl.BlockSpec((e, d), lambda i: (0, 0))]
    out_specs = [pl.BlockSpec((tm * pitch, LANES), lambda i: (i, 0)),
                 pl.BlockSpec((1, e, tm), lambda i: (i // per, 0, i % per))]
    out_shape = [jax.ShapeDtypeStruct((r * pitch, LANES), F32), jax.ShapeDtypeStruct((nbatch, e, rows_per_mod), F32)]
    args = (x, gate, g1.reshape(1, d), sh, sc, wrt)
    return in_specs, out_specs, out_shape, args


def ab_out(o_f, o_b, p, nl, onorm_g, w_out, x, gate, g1, sh, sc, wrt, *, rows_per_mod, tm, packed):
    tm = min(tm, rows_per_mod)
    r = o_f.shape[0]
    wd = HGRN_WIDTH
    row = lambda i: (i, 0)
    d = wrt.shape[1]
    c_in, c_out, c_shape, c_args = _out_common(x, gate, g1, sh, sc, wrt, rows=r, rows_per_mod=rows_per_mod, tm=tm,
                                               packed=packed)
    in_specs = [pl.BlockSpec((tm, wd), row), pl.BlockSpec((tm, wd), row),
                pl.BlockSpec((tm, wd), lambda i: (i, 4)),
                pl.BlockSpec((tm, NA_WIDTH), row),
                pl.BlockSpec((1, HGRN_DK), lambda i: (0, 0)),
                pl.BlockSpec(w_out.shape, lambda i: (0, 0))] + c_in
    return pl.pallas_call(
        functools.partial(_ab_out_kernel, packed=(tm, d) if packed else None),
        grid=(r // tm,), in_specs=in_specs, out_specs=c_out, out_shape=c_shape,
        compiler_params=_cparams("parallel"), name="ab_out",
    )(o_f, o_b, p, nl, onorm_g.reshape(1, HGRN_DK), w_out, *c_args)


def ssd_out(y_f, y_b, xa, p, dsk, ng, w_out, x, gate, g1, sh, sc, wrt, *, rows_per_mod, tm, packed):
    tm = min(tm, rows_per_mod)
    r = y_f.shape[0]
    inner = w_out.shape[0]
    row = lambda i: (i, 0)
    d = wrt.shape[1]
    c_in, c_out, c_shape, c_args = _out_common(x, gate, g1, sh, sc, wrt, rows=r, rows_per_mod=rows_per_mod, tm=tm,
                                               packed=packed)
    in_specs = [pl.BlockSpec((tm, inner), row), pl.BlockSpec((tm, inner), row),
                pl.BlockSpec((tm, inner), row),
                pl.BlockSpec((tm, inner), row),
                pl.BlockSpec((1, inner), lambda i: (0, 0)), pl.BlockSpec((1, inner), lambda i: (0, 0)),
                pl.BlockSpec(w_out.shape, lambda i: (0, 0))] + c_in
    return pl.pallas_call(
        functools.partial(_ssd_out_kernel, group_width=inner // SSD_GROUPS, packed=(tm, d) if packed else None),
        grid=(r // tm,), in_specs=in_specs, out_specs=c_out, out_shape=c_shape,
        compiler_params=_cparams("parallel"), name="ssd_out",
    )(y_f, y_b, xa, p, dsk.reshape(1, inner), ng.reshape(1, inner), w_out, *c_args)


F32_INF_BITS = 0x7F800000


def _lane_cumsum(src_ref, dst_ref, n_chunks):
    e = src_ref.shape[0]
    upper = _tri(LANES, True).astype(BF16)

    def body(j, off):
        ds = pl.ds(pl.multiple_of(j * LANES, LANES), LANES)
        c = _dot(src_ref[:, ds].astype(BF16), upper) + off
        dst_ref[:, ds] = c
        return c[:, LANES - 1:LANES]

    lax.fori_loop(0, n_chunks, body, jnp.zeros((e, 1), F32))


def _select_kernel(aff_ref, idx_ref, gate_ref, m_scr, c_scr, *, seq, cap, nph, kc, row_scale):
    e = aff_ref.shape[1]
    n_chunks = seq // LANES
    bits = pltpu.bitcast(aff_ref[0], I32)

    def bisect(_, lohi):
        lo, hi = lohi
        mid = lo + lax.shift_right_logical(hi - lo, 1)
        cnt = jnp.sum((bits >= mid).astype(F32), axis=1, keepdims=True)
        ge = cnt >= cap
        return jnp.where(ge, mid, lo), jnp.where(ge, hi, mid)

    thr, _ = lax.fori_loop(0, 31, bisect, (jnp.zeros((e, 1), I32), jnp.full((e, 1), F32_INF_BITS, I32)))
    gt = bits > thr
    eq = bits == thr
    need = cap - jnp.sum(gt.astype(F32), axis=1, keepdims=True)
    m_scr[...] = eq.astype(F32)
    _lane_cumsum(m_scr, c_scr, n_chunks)
    sel = gt | (eq & (c_scr[...] <= need))
    m_scr[...] = sel.astype(F32)
    _lane_cumsum(m_scr, c_scr, n_chunks)

    sub_p = lax.broadcasted_iota(I32, (nph, kc), 0).astype(F32)
    sub_l = lax.broadcasted_iota(I32, (LANES, kc), 0).astype(F32)
    lane_t = lax.broadcasted_iota(I32, (1, kc), 1)
    inv = 1.0 / LANES
    for ex in range(e):
        def body(c, acc):
            ds = pl.ds(pl.multiple_of(c * kc, LANES), kc)
            pos = c_scr[ex:ex + 1, ds] - 1.0
            phi = jnp.floor(pos * inv)
            plo = pos - LANES * phi
            hit = (phi == sub_p) & (m_scr[ex:ex + 1, ds] > 0.0)
            t = (lane_t + c * kc).astype(F32)
            th = jnp.floor(t * inv)
            tl = t - LANES * th
            a1, a2, a3 = _split3(aff_ref[0, ex:ex + 1, ds])
            rows = [th, tl, a1.astype(F32), a2.astype(F32), a3.astype(F32)]
            lhs = jnp.concatenate([jnp.where(hit, r, 0.0) for r in rows], axis=0).astype(BF16)
            onehot = (plo == sub_l).astype(BF16)
            return acc + _dot_nt(lhs, onehot)

        acc = lax.fori_loop(0, seq // kc, body, jnp.zeros((5 * nph, LANES), F32))
        token = (acc[0:nph] * LANES + acc[nph:2 * nph]).astype(I32)
        idx_ref[0, ex] = (pl.program_id(0) * seq + token) * row_scale
        gate_ref[0, ex] = acc[2 * nph:3 * nph] + acc[3 * nph:4 * nph] + acc[4 * nph:5 * nph]


def expert_select(aff_t, cap, row_scale):
    bsz, e, seq = aff_t.shape
    nph = -(-cap // LANES)
    nph = -(-nph // SUBLANES) * SUBLANES
    kc = min(2048, seq)
    idx, gate = pl.pallas_call(
        functools.partial(_select_kernel, seq=seq, cap=cap, nph=nph, kc=kc, row_scale=row_scale),
        grid=(bsz,),
        in_specs=[pl.BlockSpec((1, e, seq), lambda b: (b, 0, 0))],
        out_specs=[pl.BlockSpec((1, e, nph, LANES), lambda b: (b, 0, 0, 0)),
                   pl.BlockSpec((1, e, nph, LANES), lambda b: (b, 0, 0, 0))],
        out_shape=[jax.ShapeDtypeStruct((bsz, e, nph, LANES), I32),
                   jax.ShapeDtypeStruct((bsz, e, nph, LANES), F32)],
        scratch_shapes=[pltpu.VMEM((e, seq), F32), pltpu.VMEM((e, seq), F32)],
        compiler_params=_cparams("parallel"),
        name="expert_select",
    )(aff_t)
    return idx.reshape(bsz, e, nph * LANES)[:, :, :cap], gate.reshape(bsz, e, nph * LANES)[:, :, :cap]


FFN_SLOTS = 3


def _ffn_kernel(idx_ref, gate_ref, g1_ref, sh_ref, sc_ref, g5_ref, w1_ref, w3_ref, w2_ref, xa_in,
                xa_hbm, buf, gsem, ssem, *, tr, nt, d):
    del xa_in
    nx = d // LANES
    pitch = _row_pitch(d)
    grp = min(8, tr)
    j = pl.program_id(2)
    step = (pl.program_id(0) * pl.num_programs(1) + pl.program_id(1)) * nt + j
    slot = step % FFN_SLOTS
    nslot = (step + 1) % FFN_SLOTS

    def start_gathers(tile, sl):
        def body(i, c):
            for k in range(grp):
                src = idx_ref[0, 0, tile * tr + i * grp + k]
                dst = i * (grp * pitch) + k * pitch
                pltpu.make_async_copy(xa_hbm.at[pl.ds(src, 2 * nx)], buf.at[sl, pl.ds(dst, 2 * nx)],
                                      gsem.at[sl]).start()
            return c
        lax.fori_loop(0, tr // grp, body, 0)

    def start_scatters(tile, sl):
        def body(i, c):
            for k in range(grp):
                dst = idx_ref[0, 0, tile * tr + i * grp + k] + nx
                src = i * (grp * pitch) + k * pitch + nx
                pltpu.make_async_copy(buf.at[sl, pl.ds(src, nx)], xa_hbm.at[pl.ds(dst, nx)], ssem.at[sl]).start()
            return c
        lax.fori_loop(0, tr // grp, body, 0)

    def wait_gathers(sl):
        n = tr * 2 * nx
        pltpu.make_async_copy(xa_hbm.at[pl.ds(0, n)], buf.at[sl, pl.ds(0, n)], gsem.at[sl]).wait()

    def wait_scatters(sl):
        n = tr * nx
        pltpu.make_async_copy(buf.at[sl, pl.ds(0, n)], xa_hbm.at[pl.ds(0, n)], ssem.at[sl]).wait()

    @pl.when(j == 0)
    def _():
        start_gathers(0, slot)

    @pl.when(j + 1 < nt)
    def _():
        @pl.when(j >= FFN_SLOTS - 1)
        def _():
            wait_scatters(nslot)

        start_gathers(j + 1, nslot)

    wait_gathers(slot)
    rows = buf.at[slot]
    f = _rms_mod(_load_packed(rows, tr, 0, d), g1_ref[...], sh_ref[0], sc_ref[0]).astype(BF16)
    hid = (_silu(_dot(f, w1_ref[0])) * _dot(f, w3_ref[0])).astype(BF16)
    y = _dot(hid, w2_ref[0])
    eye = lax.broadcasted_iota(I32, (tr, tr), 0) == lax.broadcasted_iota(I32, (tr, tr), 1)
    gcol = jnp.sum(jnp.where(eye, gate_ref[0], 0.0), axis=1, keepdims=True)
    _store_packed(rows, _load_packed(rows, tr, 1, d) + (g5_ref[0] * gcol) * y, tr, 1, d)
    start_scatters(j, slot)

    @pl.when(j == nt - 1)
    def _():
        for back in range(min(FFN_SLOTS, nt)):
            wait_scatters((step - back) % FFN_SLOTS)


def expert_ffn_combine(xa, idx, gate, g1, sh, sc, g5, w1, w3, w2):
    bsz, e, cap = idx.shape
    d, ff = w1.shape[1], w1.shape[2]
    tr = min(256, cap)
    nt = cap // tr
    mod = lambda b, ex, j: (b, 0, 0)
    return pl.pallas_call(
        functools.partial(_ffn_kernel, tr=tr, nt=nt, d=d),
        grid=(bsz, e, nt),
        in_specs=[pl.BlockSpec((1, 1, cap), lambda b, ex, j: (b * e + ex, 0, 0), memory_space=pltpu.SMEM),
                  pl.BlockSpec((1, 1, tr), lambda b, ex, j: ((b * e + ex) * nt + j, 0, 0)),
                  pl.BlockSpec((1, d), lambda b, ex, j: (0, 0)),
                  pl.BlockSpec((1, 1, d), mod), pl.BlockSpec((1, 1, d), mod), pl.BlockSpec((1, 1, d), mod),
                  pl.BlockSpec((1, d, ff), lambda b, ex, j: (ex, 0, 0)),
                  pl.BlockSpec((1, d, ff), lambda b, ex, j: (ex, 0, 0)),
                  pl.BlockSpec((1, ff, d), lambda b, ex, j: (ex, 0, 0)),
                  pl.BlockSpec(memory_space=pl.ANY)],
        out_specs=pl.BlockSpec(memory_space=pl.ANY),
        out_shape=jax.ShapeDtypeStruct(xa.shape, F32),
        scratch_shapes=[pltpu.VMEM((FFN_SLOTS, tr * _row_pitch(d), LANES), F32),
                        pltpu.SemaphoreType.DMA((FFN_SLOTS,)), pltpu.SemaphoreType.DMA((FFN_SLOTS,))],
        input_output_aliases={9: 0},
        compiler_params=_cparams("arbitrary", "arbitrary", "arbitrary"),
        name="expert_ffn_combine",
    )(idx.reshape(bsz * e, 1, cap), gate.reshape(bsz * e * nt, 1, tr), g1.reshape(1, d), sh, sc, g5,
      w1, w3, w2, xa)


CONV_HALO = 8


def _ssd_prep_kernel(cur_ref, prev_ref, next_ref, w_ref, b_ref, dt_ref, dtb_ref, xa_ref, dtv_ref, ext_ref,
                     *, tm, per):
    i = pl.program_id(0)
    first = (i % per) == 0
    last = (i % per) == per - 1
    ext_ref[0:CONV_HALO, :] = jnp.where(first, 0.0, prev_ref[...])
    ext_ref[CONV_HALO:CONV_HALO + tm, :] = cur_ref[...]
    ext_ref[CONV_HALO + tm:2 * CONV_HALO + tm, :] = jnp.where(last, 0.0, next_ref[...])
    acc = b_ref[...] + w_ref[0:1, :] * ext_ref[pl.ds(CONV_HALO - SSD_CONV // 2, tm), :]
    for k in range(1, SSD_CONV):
        acc = acc + w_ref[k:k + 1, :] * ext_ref[pl.ds(CONV_HALO - SSD_CONV // 2 + k, tm), :]
    xa_ref[...] = _silu(acc)
    v = dt_ref[...] + dtb_ref[...]
    dtv_ref[...] = jnp.maximum(v, 0.0) + jnp.log1p(jnp.exp(-jnp.abs(v)))


def ssd_prep(p, conv_w, conv_b, dt_bias, *, rows_per_seq, tm, col_xbc, col_dt):
    r = p.shape[0]
    cdim = conv_w.shape[1]
    half = cdim // 2
    tm = min(tm, rows_per_seq)
    per = rows_per_seq // tm
    hb = tm // CONV_HALO
    nhb = r // CONV_HALO
    c0 = col_xbc // half
    return pl.pallas_call(
        functools.partial(_ssd_prep_kernel, tm=tm, per=per),
        grid=(r // tm, 2),
        in_specs=[pl.BlockSpec((tm, half), lambda i, j: (i, c0 + j)),
                  pl.BlockSpec((CONV_HALO, half), lambda i, j: (jnp.maximum(i * hb - 1, 0), c0 + j)),
                  pl.BlockSpec((CONV_HALO, half), lambda i, j: (jnp.minimum((i + 1) * hb, nhb - 1), c0 + j)),
                  pl.BlockSpec((SSD_CONV, half), lambda i, j: (0, j)),
                  pl.BlockSpec((1, half), lambda i, j: (0, j)),
                  pl.BlockSpec((tm, LANES), lambda i, j: (i, col_dt // LANES)),
                  pl.BlockSpec((1, LANES), lambda i, j: (0, 0))],
        out_specs=[pl.BlockSpec((tm, half), lambda i, j: (i, j)),
                   pl.BlockSpec((tm, LANES), lambda i, j: (i, 0))],
        out_shape=[jax.ShapeDtypeStruct((r, cdim), F32), jax.ShapeDtypeStruct((r, LANES), F32)],
        scratch_shapes=[pltpu.VMEM((tm + 2 * CONV_HALO, half), F32)],
        compiler_params=_cparams("parallel", "arbitrary"),
        name="ssd_prep",
    )(p, p, p, conv_w, conv_b.reshape(1, cdim), p, dt_bias)


SSD_HPG = 4


def _ssd_block(x, bm, cm, dtv, a_row, hs, lane0, upper, need_y=True):
    t = x.shape[0]
    hp = x.shape[1]
    pdim = hp // SSD_HPG
    ch = SSD_CHUNK
    nc = t // ch
    expand = (lax.broadcasted_iota(I32, (LANES, hp), 0) ==
              lane0 + lax.broadcasted_iota(I32, (LANES, hp), 1) // pdim).astype(BF16)
    dtx = sum(_dot(p, expand) for p in _split3(dtv))
    a_x = sum(_dot(p, expand) for p in _split3(jnp.broadcast_to(a_row, (SUBLANES, LANES))))[0:1]
    dtax = dtx * a_x
    c3 = _dot(_block_tri(t, ch, upper).astype(BF16), jnp.concatenate(_split3(dtax), axis=1))
    acx = c3[:, 0:hp] + c3[:, hp:2 * hp] + c3[:, 2 * hp:3 * hp]
    r = lax.broadcasted_iota(I32, (ch, hp), 0)
    s = lax.broadcasted_iota(I32, (ch, hp), 1) % ch
    src_le = (s <= r) if upper else (s >= r)
    ident = s == r
    causal = (r <= s) if upper else (r >= s)
    same_head = (lax.broadcasted_iota(I32, (hp, hp), 0) // pdim) == (lax.broadcasted_iota(I32, (hp, hp), 1) // pdim)
    ydiag, decay_out, alasts, upds = [], [], [], []
    for c in range(nc):
        rs = slice(c * ch, (c + 1) * ch)
        acx_c, dtx_c, x_c = acx[rs], dtx[rs], x[rs]
        bm_c = bm[rs].astype(BF16)
        alast = acx_c[0:1] if upper else acx_c[ch - 1:ch]
        if need_y:
            arx = jnp.sum(jnp.where(src_le, dtax[rs], 0.0), axis=0, keepdims=True)
            dtr = jnp.sum(jnp.where(ident, dtx_c, 0.0), axis=0, keepdims=True)
            dec = jnp.where(causal, jnp.exp(jnp.minimum(acx_c - arx, 0.0)), 0.0)
            cbx = _dot_nt(cm[rs].astype(BF16), jnp.concatenate([bm_c] * SSD_HPG, axis=0))
            gx = (cbx * dec * dtr).astype(BF16)
            xbd = jnp.where(same_head, jnp.concatenate([x_c] * SSD_HPG, axis=0), 0.0).astype(BF16)
            ydiag.append(_dot(gx, xbd))
            decay_out.append(jnp.exp(acx_c))
        alasts.append(alast)
        upds.append(_dot_tn(bm_c, (x_c * (jnp.exp(alast - acx_c) * dtx_c)).astype(BF16)))
    ys = [None] * nc
    for c in (reversed(range(nc)) if upper else range(nc)):
        if need_y:
            ys[c] = ydiag[c] + _dot(cm[c * ch:(c + 1) * ch].astype(BF16), hs.astype(BF16)) * decay_out[c]
        hs = hs * jnp.exp(alasts[c]) + upds[c]
    return (jnp.concatenate(ys, axis=0) if need_y else None), hs


def _ssd_kernel(xf_ref, bf_ref, cf_ref, df_ref, xb_ref, bb_ref, cb_ref, db_ref,
                xc_ref, bc_ref, cc_ref, dc_ref, alog_ref, yf_ref, yb_ref, hf_ref, hb_ref):
    g = pl.program_id(0) % SSD_GROUPS
    a_row = -jnp.exp(alog_ref[...])
    lane_f = g * SSD_HPG
    lane_b = SSD_GROUPS * SSD_HPG + g * SSD_HPG

    def run(x_ref, b_ref, c_ref, d_ref, y_ref, hs, lane0, upper):
        n = x_ref.shape[0]
        sub = min(SCAN_SUB, n)
        order = range(n // sub)
        for sb in (reversed(order) if upper else order):
            rs = slice(sb * sub, (sb + 1) * sub)
            y, hs = _ssd_block(x_ref[rs, :], b_ref[rs, :], c_ref[rs, :], d_ref[rs, :], a_row, hs, lane0, upper,
                               need_y=y_ref is not None)
            if y_ref is not None:
                y_ref[rs, :] = y
        return hs

    @pl.when(pl.program_id(1) == 0)
    def _():
        z0 = jnp.zeros(hf_ref.shape, F32)
        hf_ref[...] = run(xc_ref, bc_ref, cc_ref, dc_ref, None, z0, lane_f, False)
        hb_ref[...] = run(xc_ref, bc_ref, cc_ref, dc_ref, None, z0, lane_b, True)

    hf_ref[...] = run(xf_ref, bf_ref, cf_ref, df_ref, yf_ref, hf_ref[...], lane_f, False)
    hb_ref[...] = run(xb_ref, bb_ref, cb_ref, db_ref, yb_ref, hb_ref[...], lane_b, True)


def ssd_scan(xa, dtv, xa_c, dtv_c, a_log_row, *, bsz, seq, lc, tb, inner):
    g = SSD_GROUPS
    n = SSD_STATE
    xw = inner // g
    nb = seq // tb
    cb0 = inner // n
    cc0 = cb0 + g

    def blk(i, rev):
        return nb - 1 - i if rev else i

    def lat(width, col, rev):
        return pl.BlockSpec((tb, width), lambda bg, i: ((bg // g) * nb + blk(i, rev), col + bg % g))

    def lat_dt(rev):
        return pl.BlockSpec((tb, LANES), lambda bg, i: ((bg // g) * nb + blk(i, rev), 0))

    def ctx(width, col):
        return pl.BlockSpec((lc, width), lambda bg, i: (bg // g, col + bg % g))

    def out(rev):
        return pl.BlockSpec((tb, xw), lambda bg, i: ((bg // g) * nb + blk(i, rev), bg % g))

    return pl.pallas_call(
        _ssd_kernel,
        grid=(bsz * g, nb),
        in_specs=[lat(xw, 0, False), lat(n, cb0, False), lat(n, cc0, False), lat_dt(False),
                  lat(xw, 0, True), lat(n, cb0, True), lat(n, cc0, True), lat_dt(True),
                  ctx(xw, 0), ctx(n, cb0), ctx(n, cc0), pl.BlockSpec((lc, LANES), lambda bg, i: (bg // g, 0)),
                  pl.BlockSpec((1, LANES), lambda bg, i: (0, 0))],
        out_specs=[out(False), out(True)],
        out_shape=[jax.ShapeDtypeStruct((bsz * seq, inner), F32), jax.ShapeDtypeStruct((bsz * seq, inner), F32)],
        scratch_shapes=[pltpu.VMEM((n, xw), F32), pltpu.VMEM((n, xw), F32)],
        compiler_params=_cparams("parallel", "arbitrary", flags=SCAN_FLAGS),
        name="ssd_scan",
    )(xa, xa, xa, dtv, xa, xa, xa, dtv, xa_c, xa_c, xa_c, dtv_c, a_log_row)


def _final_norm_kernel(x_ref, g_ref, o_ref, *, packed):
    x = _load_stream(x_ref, packed)
    o_ref[...] = x * lax.rsqrt(jnp.mean(x * x, axis=-1, keepdims=True) + RMS_EPS) * g_ref[...]


def final_norm(x, g, *, rows, tm, packed):
    r = rows
    d = g.shape[0]
    return pl.pallas_call(
        functools.partial(_final_norm_kernel, packed=(tm, d) if packed else None),
        grid=(r // tm,),
        in_specs=[_stream_spec(tm, d, packed, lambda i: i), pl.BlockSpec((1, d), lambda i: (0, 0))],
        out_specs=pl.BlockSpec((tm, d), lambda i: (i, 0)),
        out_shape=jax.ShapeDtypeStruct((r, d), F32),
        compiler_params=_cparams("parallel"),
        name="final_norm",
    )(x, g.reshape(1, d))


def _moe(xa, aff, g1, sh, sc, g5, w1, w3, w2, *, seq):
    cap = max(1, CAPACITY_FACTOR * seq // N_EXPERTS)
    idx, gate = expert_select(aff, cap, _row_pitch(w1.shape[1]))
    return expert_ffn_combine(xa, idx, gate, g1, sh, sc, g5, w1, w3, w2)


def kernel(x, c, ctx, c_ctx, ada_w, ada_b, norm_g, final_g, ab_w_in, ab_w_out, hgrn_lb_logits, hgrn_onorm_g, na_rpb, ssd_w_in, ssd_conv_w, ssd_conv_b, ssd_a_log, ssd_dt_bias, ssd_d, ssd_norm_g, ssd_w_out, moe_router, moe_w1, moe_w3, moe_w2):
    bsz, seq, d = x.shape
    lc = ctx.shape[1]
    depth = ada_w.shape[0]
    assert depth == 2, "context outputs of an SSD layer are not implemented (only needed when a layer follows it)"
    mods = modulation_table(c, c_ctx, ada_w, ada_b)
    lb_all = jnp.cumsum(jax.nn.softmax(hgrn_lb_logits.astype(F32), axis=1), axis=1)
    xl = x.reshape(bsz * seq, d)
    xc = ctx.reshape(bsz * lc, d)
    packed = False
    hg = 5 * HGRN_WIDTH
    for l in range(depth):
        need_ctx = l < depth - 1
        k = l // 2
        m = mods[l]
        ml = [m[:bsz, j * d:(j + 1) * d].reshape(bsz, 1, d) for j in range(6)]
        mc = [jnp.broadcast_to(m[bsz, j * d:(j + 1) * d].reshape(1, 1, d), (bsz, 1, d)) for j in range(6)]
        wrt = moe_router[l].T
        w1, w3, w2 = (w[l].astype(BF16) for w in (moe_w1, moe_w3, moe_w2))
        g0, g1 = norm_g[l, 0], norm_g[l, 1]
        if l % 2 == 0:
            w_in = ab_w_in[k].astype(BF16)
            w_out = ab_w_out[k].astype(BF16)
            p_lat = norm_mod_project(xl, g0, ml[0], ml[1], w_in, rows=bsz * seq, rows_per_mod=seq, tm=512, tn=1024,
                                     packed=packed)
            p_ctx = norm_mod_project(xc, g0, mc[0], mc[1], w_in, rows=bsz * lc, rows_per_mod=lc, tm=512, tn=1024,
                                     packed=packed)
            o_f, o_b, oc_f, oc_b = hgrn_scan(p_lat, p_ctx, lb_all[:, k], bsz=bsz, seq=seq, lc=lc, tb=min(512, seq))
            nl = neighborhood_attention(p_lat, p_ctx, na_bias_table(na_rpb[k]), bsz=bsz, seq=seq, lc=lc, col0=hg)
            xa, aff = ab_out(o_f, o_b, p_lat, nl, hgrn_onorm_g[k], w_out, xl, ml[2], g1, ml[3], ml[4], wrt,
                             rows_per_mod=seq, tm=512, packed=packed)
            if need_ctx:
                nc = context_attention(p_ctx, bsz=bsz, lc=lc, col0=hg)
                ca, caff = ab_out(oc_f, oc_b, p_ctx, nc, hgrn_onorm_g[k], w_out, xc, mc[2], g1, mc[3], mc[4],
                                  wrt, rows_per_mod=lc, tm=512, packed=packed)
        else:
            inner = ssd_w_out.shape[1]
            cdim = ssd_conv_w.shape[2]
            heads2 = ssd_dt_bias.shape[1] * ssd_dt_bias.shape[2]
            wz = ssd_w_in[k]
            pad = jnp.zeros((d, LANES - heads2), F32)
            w_in = jnp.concatenate([wz, pad], axis=1).astype(BF16)
            w_out = ssd_w_out[k].astype(BF16)
            tn = w_in.shape[1] // 7
            p_lat = norm_mod_project(xl, g0, ml[0], ml[1], w_in, rows=bsz * seq, rows_per_mod=seq, tm=256, tn=tn,
                                     packed=packed)
            p_ctx = norm_mod_project(xc, g0, mc[0], mc[1], w_in, rows=bsz * lc, rows_per_mod=lc, tm=256, tn=tn,
                                     packed=packed)
            dtb = jnp.zeros((1, LANES), F32).at[0, :heads2].set(ssd_dt_bias[k].reshape(-1))
            alog = jnp.zeros((1, LANES), F32).at[0, :heads2].set(ssd_a_log[k].reshape(-1))
            prep = functools.partial(ssd_prep, conv_w=ssd_conv_w[k], conv_b=ssd_conv_b[k], dt_bias=dtb, tm=256,
                                     col_xbc=inner, col_dt=inner + cdim)
            xs, dtv = prep(p_lat, rows_per_seq=seq)
            xs_c, dtv_c = prep(p_ctx, rows_per_seq=lc)
            y_f, y_b = ssd_scan(xs, dtv, xs_c, dtv_c, alog, bsz=bsz, seq=seq, lc=lc, tb=min(512, seq), inner=inner)
            dsk = jnp.repeat(ssd_d[k].astype(F32), SSD_HEADDIM)
            xa, aff = ssd_out(y_f, y_b, xs, p_lat, dsk, ssd_norm_g[k], w_out, xl, ml[2], g1, ml[3], ml[4], wrt,
                              rows_per_mod=seq, tm=256, packed=packed)
        xl = _moe(xa, aff, g1, ml[3], ml[4], ml[5], w1, w3, w2, seq=seq)
        if need_ctx:
            xc = _moe(ca, caff, g1, mc[3], mc[4], mc[5], w1, w3, w2, seq=lc)
        packed = True
    return final_norm(xl, final_g, rows=bsz * seq, tm=512, packed=packed).reshape(bsz, seq, d)
```

```python
import functools
import math

import jax
import jax.numpy as jnp
import numpy as np
from jax import lax
from jax.experimental import pallas as pl
from jax.experimental.pallas import tpu as pltpu

F32 = jnp.float32
BF16 = jnp.bfloat16
I32 = jnp.int32

GRID_W = 64
HGRN_HEADS = 4
HGRN_DK = 128
HGRN_WIDTH = HGRN_HEADS * HGRN_DK
GLA_CHUNK = 32
NA_HEADS = 8
NA_DH = 64
NA_WIDTH = NA_HEADS * NA_DH
NA_KR = 8
NA_KC = 16
SSD_HEADDIM = 64
SSD_GROUPS = 8
SSD_STATE = 128
SSD_CONV = 5
SSD_CHUNK = 64
N_EXPERTS = 16
CAPACITY_FACTOR = 2
RMS_EPS = 1e-6
NEG_INF = -1e30

LANES = 128
SUBLANES = 8
VMEM_LIMIT_BYTES = 56 * 1024 * 1024


def _cparams(*sem, flags=None):
    return pltpu.CompilerParams(dimension_semantics=sem, vmem_limit_bytes=VMEM_LIMIT_BYTES, flags=flags)


SCAN_FLAGS = None


def _sigmoid(x):
    return 1.0 / (1.0 + jnp.exp(-x))


def _silu(x):
    return x * _sigmoid(x)


def _split3(x):
    a = x.astype(BF16)
    r = x - a.astype(F32)
    b = r.astype(BF16)
    c = (r - b.astype(F32)).astype(BF16)
    return a, b, c


def _dot(a, b):
    return jnp.dot(a, b, preferred_element_type=F32)


def _dot_nt(a, b):
    return lax.dot_general(a, b, (((1,), (1,)), ((), ())), preferred_element_type=F32)


def _dot_tn(a, b):
    return lax.dot_general(a, b, (((0,), (0,)), ((), ())), preferred_element_type=F32)


def _dot_exact_lhs(m01, x):
    m = m01.astype(BF16)
    a, b, c = _split3(x)
    return _dot(m, a) + _dot(m, b) + _dot(m, c)


def _dot3(a, b):
    a1 = a.astype(BF16)
    a2 = (a - a1.astype(F32)).astype(BF16)
    b1 = b.astype(BF16)
    b2 = (b - b1.astype(F32)).astype(BF16)
    return _dot(a1, b1) + _dot(a1, b2) + _dot(a2, b1)


def _dot3_nt(a, b):
    a1 = a.astype(BF16)
    a2 = (a - a1.astype(F32)).astype(BF16)
    b1 = b.astype(BF16)
    b2 = (b - b1.astype(F32)).astype(BF16)
    return _dot_nt(a1, b1) + _dot_nt(a1, b2) + _dot_nt(a2, b1)


def _rms_mod(x, g, shift, scale):
    y = x * lax.rsqrt(jnp.mean(x * x, axis=-1, keepdims=True) + RMS_EPS) * g
    return y * (1.0 + scale) + shift


def _row_pitch(d):
    rows = 2 * (d // LANES)
    return rows + (4 - rows % SUBLANES) % SUBLANES


def _load_packed(ref, n, part, d):
    nx = d // LANES
    pitch = _row_pitch(d)
    return jnp.concatenate([ref[pl.ds(part * nx + k, n, stride=pitch), :] for k in range(nx)], axis=1)


def _store_packed(ref, val, n, part, d):
    nx = d // LANES
    pitch = _row_pitch(d)
    for k in range(nx):
        ref[pl.ds(part * nx + k, n, stride=pitch), :] = val[:, k * LANES:(k + 1) * LANES]


def _load_stream(x_ref, packed):
    if not packed:
        return x_ref[...]
    n, d = packed
    return _load_packed(x_ref, n, 1, d)


def _stream_spec(tm, d, packed, index_map_row):
    if packed:
        return pl.BlockSpec((tm * _row_pitch(d), LANES), lambda *a: (index_map_row(*a), 0))
    return pl.BlockSpec((tm, d), lambda *a: (index_map_row(*a), 0))


def _mod_kernel(s_ref, w_ref, b_ref, o_ref):
    s = _silu(s_ref[...])
    o_ref[0] = _dot3(s, w_ref[0]) + b_ref[0]


def modulation_table(c, c_ctx, ada_w, ada_b):
    depth, d, n = ada_w.shape
    bsz = c.shape[0]
    s = jnp.zeros((SUBLANES, d), F32).at[:bsz].set(c).at[bsz].set(c_ctx)
    tn = 1024
    return pl.pallas_call(
        _mod_kernel,
        grid=(depth, n // tn),
        in_specs=[pl.BlockSpec((SUBLANES, d), lambda l, j: (0, 0)),
                  pl.BlockSpec((1, d, tn), lambda l, j: (l, 0, j)),
                  pl.BlockSpec((1, 1, tn), lambda l, j: (l, 0, j))],
        out_specs=pl.BlockSpec((1, SUBLANES, tn), lambda l, j: (l, 0, j)),
        out_shape=jax.ShapeDtypeStruct((depth, SUBLANES, n), F32),
        compiler_params=_cparams("parallel", "parallel"),
        name="mod_table",
    )(s, ada_w, ada_b.reshape(depth, 1, n))


def _proj_kernel(x_ref, g_ref, sh_ref, sc_ref, w_ref, o_ref, *, tn, packed):
    h = _rms_mod(_load_stream(x_ref, packed), g_ref[...], sh_ref[0], sc_ref[0]).astype(BF16)
    for j in range(w_ref.shape[1] // tn):
        o_ref[:, j * tn:(j + 1) * tn] = _dot(h, w_ref[:, j * tn:(j + 1) * tn])


def norm_mod_project(x, g, shift, scale, w, *, rows, rows_per_mod, tm, tn, packed):
    r = rows
    d, n = w.shape
    tm = min(tm, rows_per_mod)
    per = rows_per_mod // tm
    return pl.pallas_call(
        functools.partial(_proj_kernel, tn=tn, packed=(tm, d) if packed else None),
        grid=(r // tm,),
        in_specs=[_stream_spec(tm, d, packed, lambda i: i),
                  pl.BlockSpec((1, d), lambda i: (0, 0)),
                  pl.BlockSpec((1, 1, d), lambda i: (i // per, 0, 0)),
                  pl.BlockSpec((1, 1, d), lambda i: (i // per, 0, 0)),
                  pl.BlockSpec((d, n), lambda i: (0, 0))],
        out_specs=pl.BlockSpec((tm, n), lambda i: (i, 0)),
        out_shape=jax.ShapeDtypeStruct((r, n), F32),
        compiler_params=_cparams("parallel"),
        name="norm_mod_project",
    )(x, g.reshape(1, d), shift, scale, w)


def _tri(n, upper):
    r = lax.broadcasted_iota(I32, (n, n), 0)
    c = lax.broadcasted_iota(I32, (n, n), 1)
    return (r <= c) if upper else (r >= c)


SCAN_SUB = 256


def _block_tri(n, chunk, upper):
    r = lax.broadcasted_iota(I32, (n, n), 0)
    c = lax.broadcasted_iota(I32, (n, n), 1)
    same = (r // chunk) == (c // chunk)
    return same & ((r <= c) if upper else (r >= c))


def _hgrn_block(q, z, v, lb, st, upper):
    n = q.shape[0]
    ch = GLA_CHUNK
    nc = n // ch
    mask = _block_tri(n, ch, upper)
    qs = _silu(q)
    f = lb + (1.0 - lb) * _sigmoid(z)
    g = jnp.log(f)
    k = 1.0 - f
    c3 = _dot(mask.astype(BF16), jnp.concatenate(_split3(g), axis=1))
    dk = g.shape[1]
    cum = c3[:, 0:dk] + c3[:, dk:2 * dk] + c3[:, 2 * dk:3 * dk]
    cls = [cum[c * ch:c * ch + 1] if upper else cum[(c + 1) * ch - 1:(c + 1) * ch] for c in range(nc)]
    clb = jnp.concatenate([jnp.broadcast_to(cl, (ch, dk)) for cl in cls], axis=0)
    q_in = (qs * jnp.exp(cum)).astype(BF16)
    k_in = (k * jnp.exp(-cum)).astype(BF16)
    k_out = (k * jnp.exp(clb - cum)).astype(BF16)
    vb = v.astype(BF16)
    attn = jnp.where(mask, _dot_nt(q_in, k_in), 0.0)
    intra = _dot(attn.astype(BF16), vb)
    dv = v.shape[1]
    vt = v.T
    chunk_of_col = lax.broadcasted_iota(I32, (1, n), 1) // ch
    vexp_t = jnp.concatenate([jnp.where(chunk_of_col == c, vt, 0.0) for c in range(nc)], axis=0).astype(BF16)
    upd = _dot(vexp_t, k_out)
    sts = [None] * nc
    for c in (reversed(range(nc)) if upper else range(nc)):
        sts[c] = st
        st = st * jnp.exp(cls[c]) + upd[c * dv:(c + 1) * dv]
    chunk_of_row = lax.broadcasted_iota(I32, (n, 1), 0) // ch
    q_exp = jnp.concatenate([jnp.where(chunk_of_row == c, q_in, jnp.zeros_like(q_in)) for c in range(nc)], axis=1)
    st_cat = jnp.concatenate(sts, axis=1).astype(BF16)
    return intra + _dot_nt(q_exp, st_cat), st


def _hgrn_kernel(qf_ref, zf_ref, vf_ref, qb_ref, zb_ref, vb_ref,
                 qc_ref, zfc_ref, zbc_ref, vc_ref, lb_ref,
                 of_ref, ob_ref, ocf_ref, ocb_ref, stf_ref, stb_ref):
    lbf = lb_ref[0:1]
    lbb = lb_ref[1:2]

    def run(q_ref, z_ref, v_ref, lb, o_ref, st, upper):
        n = q_ref.shape[0]
        sub = min(SCAN_SUB, n)
        order = range(n // sub)
        for sb in (reversed(order) if upper else order):
            rs = slice(sb * sub, (sb + 1) * sub)
            o_ref[rs, :], st = _hgrn_block(q_ref[rs, :], z_ref[rs, :], v_ref[rs, :], lb, st, upper)
        return st

    @pl.when(pl.program_id(1) == 0)
    def _():
        z0 = jnp.zeros((HGRN_DK, HGRN_DK), F32)
        stf_ref[...] = run(qc_ref, zfc_ref, vc_ref, lbf, ocf_ref, z0, False)
        stb_ref[...] = run(qc_ref, zbc_ref, vc_ref, lbb, ocb_ref, z0, True)

    stf_ref[...] = run(qf_ref, zf_ref, vf_ref, lbf, of_ref, stf_ref[...], False)
    stb_ref[...] = run(qb_ref, zb_ref, vb_ref, lbb, ob_ref, stb_ref[...], True)


def hgrn_scan(p_lat, p_ctx, lb, *, bsz, seq, lc, tb):
    h = HGRN_HEADS
    dk = HGRN_DK
    nb = seq // tb
    cq, czf, czb, cv = 0, h, 2 * h, 3 * h

    def lat(col, rev):
        if rev:
            return pl.BlockSpec((tb, dk), lambda bh, i: ((bh // h) * nb + nb - 1 - i, col + bh % h))
        return pl.BlockSpec((tb, dk), lambda bh, i: ((bh // h) * nb + i, col + bh % h))

    def ctx(col):
        return pl.BlockSpec((lc, dk), lambda bh, i: (bh // h, col + bh % h))

    def out_lat(rev):
        if rev:
            return pl.BlockSpec((tb, dk), lambda bh, i: ((bh // h) * nb + nb - 1 - i, bh % h))
        return pl.BlockSpec((tb, dk), lambda bh, i: ((bh // h) * nb + i, bh % h))

    out_ctx = pl.BlockSpec((lc, dk), lambda bh, i: (bh // h, bh % h))
    w = h * dk
    return pl.pallas_call(
        _hgrn_kernel,
        grid=(bsz * h, nb),
        in_specs=[lat(cq, False), lat(czf, False), lat(cv, False),
                  lat(cq, True), lat(czb, True), lat(cv, True),
                  ctx(cq), ctx(czf), ctx(czb), ctx(cv),
                  pl.BlockSpec((2, dk), lambda bh, i: (0, bh % h))],
        out_specs=[out_lat(False), out_lat(True), out_ctx, out_ctx],
        out_shape=[jax.ShapeDtypeStruct((bsz * seq, w), F32), jax.ShapeDtypeStruct((bsz * seq, w), F32),
                   jax.ShapeDtypeStruct((bsz * lc, w), F32), jax.ShapeDtypeStruct((bsz * lc, w), F32)],
        scratch_shapes=[pltpu.VMEM((dk, dk), F32), pltpu.VMEM((dk, dk), F32)],
        compiler_params=_cparams("parallel", "arbitrary", flags=SCAN_FLAGS),
        name="hgrn_scan",
    )(p_lat, p_lat, p_lat, p_lat, p_lat, p_lat, p_ctx, p_ctx, p_ctx, p_ctx, lb)


NA_RB = 8


def na_bias_table(rpb):
    w = GRID_W
    qc = np.arange(w)
    win0 = np.clip(qc - NA_KC // 2, 0, w - NA_KC)
    kc = np.arange(w)
    ok = (kc[None, :] >= win0[:, None]) & (kc[None, :] < win0[:, None] + NA_KC)
    dc = np.clip(kc[None, :] - qc[:, None] + NA_KC - 1, 0, 2 * NA_KC - 2)
    v = np.arange(NA_KR)
    j = np.arange(NA_KR)
    dr = j[None, :] - v[:, None] + NA_KR - 1
    t = rpb.astype(F32)[:, dr][:, :, :, dc]
    t = jnp.where(ok[None, None, None], t, NEG_INF)
    t = jnp.transpose(t, (0, 1, 3, 2, 4))
    return t.reshape(rpb.shape[0], NA_KR, w, NA_KR * w)


def _na_kernel(q_ref, kp_ref, kc_ref, kn_ref, vp_ref, vc_ref, vn_ref, kx_ref, vx_ref, bias_ref,
               o_ref, ks_ref, vs_ref, *, rows):
    w = GRID_W
    blk = NA_RB * w
    i = pl.program_id(2)
    ks_ref[0:blk, :] = kp_ref[...].astype(BF16)
    ks_ref[blk:2 * blk, :] = kc_ref[...].astype(BF16)
    ks_ref[2 * blk:3 * blk, :] = kn_ref[...].astype(BF16)
    vs_ref[0:blk, :] = vp_ref[...].astype(BF16)
    vs_ref[blk:2 * blk, :] = vc_ref[...].astype(BF16)
    vs_ref[2 * blk:3 * blk, :] = vn_ref[...].astype(BF16)
    scale = NA_DH ** -0.5
    first_head = lax.broadcasted_iota(I32, (w, 2 * NA_DH), 1) < NA_DH
    qbd = []
    for rr in range(NA_RB):
        q2 = q_ref[rr * w:(rr + 1) * w, :] * scale
        qbd.append(jnp.concatenate([jnp.where(first_head, q2, 0.0), jnp.where(first_head, 0.0, q2)],
                                   axis=0).astype(BF16))
    sx = _dot_nt(jnp.concatenate(qbd, axis=0), kx_ref[...].astype(BF16))
    mx = jnp.max(sx, axis=-1, keepdims=True)
    px = jnp.exp(sx - mx)
    dx = jnp.sum(px, axis=-1, keepdims=True)
    ox = _dot(px.astype(BF16), vx_ref[...].astype(BF16))
    for rr in range(NA_RB):
        r = i * NA_RB + rr
        r0 = jnp.clip(r - NA_KR // 2, 0, rows - NA_KR)
        start = pl.multiple_of((r0 - (i - 1) * NA_RB) * w, w)
        vv = r - r0
        s = _dot_nt(qbd[rr], ks_ref[pl.ds(start, NA_KR * w), :])
        s = s + jnp.concatenate([bias_ref[0, vv], bias_ref[1, vv]], axis=0)
        mw = jnp.max(s, axis=-1, keepdims=True)
        p = jnp.exp(s - mw)
        dw = jnp.sum(p, axis=-1, keepdims=True)
        ow = _dot(p.astype(BF16), vs_ref[pl.ds(start, NA_KR * w), :])
        rs = slice(rr * 2 * w, (rr + 1) * 2 * w)
        m = jnp.maximum(mw, mx[rs])
        aw = jnp.exp(mw - m)
        ax = jnp.exp(mx[rs] - m)
        o = (ow * aw + ox[rs] * ax) / (dw * aw + dx[rs] * ax)
        o_ref[rr * w:(rr + 1) * w, :] = jnp.where(first_head, o[0:w], o[w:2 * w]).astype(o_ref.dtype)


def neighborhood_attention(p_lat, p_ctx, bias_tbl, *, bsz, seq, lc, col0):
    w = GRID_W
    rows = seq // w
    blk = NA_RB * w
    nb = rows // NA_RB
    hp = NA_HEADS // 2
    pw = 2 * NA_DH
    cq = col0 // pw
    ck = cq + hp
    cv = ck + hp

    def lat(col, off):
        def im(b, h, i):
            return (b * nb + jnp.clip(i + off, 0, nb - 1), col + h)
        return pl.BlockSpec((blk, pw), im)

    def ctx(col):
        return pl.BlockSpec((lc, pw), lambda b, h, i: (b, col + h))

    return pl.pallas_call(
        functools.partial(_na_kernel, rows=rows),
        grid=(bsz, hp, nb),
        in_specs=[lat(cq, 0), lat(ck, -1), lat(ck, 0), lat(ck, 1), lat(cv, -1), lat(cv, 0), lat(cv, 1),
                  ctx(ck), ctx(cv),
                  pl.BlockSpec((2, NA_KR, w, NA_KR * w), lambda b, h, i: (h, 0, 0, 0))],
        out_specs=pl.BlockSpec((blk, pw), lambda b, h, i: (b * nb + i, h)),
        out_shape=jax.ShapeDtypeStruct((bsz * seq, NA_WIDTH), BF16),
        scratch_shapes=[pltpu.VMEM((3 * blk, pw), BF16), pltpu.VMEM((3 * blk, pw), BF16)],
        compiler_params=_cparams("parallel", "parallel", "arbitrary"),
        name="neighborhood_attention",
    )(p_lat, p_lat, p_lat, p_lat, p_lat, p_lat, p_lat, p_ctx, p_ctx, bias_tbl)


def _ctx_attn_kernel(q_ref, k_ref, v_ref, o_ref):
    scale = NA_DH ** -0.5
    outs = []
    for hh in range(2):
        cs = slice(hh * NA_DH, (hh + 1) * NA_DH)
        q = (q_ref[:, cs] * scale).astype(BF16)
        s = _dot_nt(q, k_ref[:, cs].astype(BF16))
        p = jnp.exp(s - jnp.max(s, axis=-1, keepdims=True))
        o = _dot(p.astype(BF16), v_ref[:, cs].astype(BF16))
        outs.append(o / jnp.sum(p, axis=-1, keepdims=True))
    o_ref[...] = jnp.concatenate(outs, axis=-1).astype(o_ref.dtype)


def context_attention(p_ctx, *, bsz, lc, col0):
    hp = NA_HEADS // 2
    pw = 2 * NA_DH
    cq = col0 // pw

    def spec(col):
        return pl.BlockSpec((lc, pw), lambda b, h: (b, col + h))

    return pl.pallas_call(
        _ctx_attn_kernel,
        grid=(bsz, hp),
        in_specs=[spec(cq), spec(cq + hp), spec(cq + 2 * hp)],
        out_specs=pl.BlockSpec((lc, pw), lambda b, h: (b, h)),
        out_shape=jax.ShapeDtypeStruct((bsz * lc, NA_WIDTH), BF16),
        compiler_params=_cparams("parallel", "parallel"),
        name="context_attention",
    )(p_ctx, p_ctx, p_ctx)


def _residual_router(x, y, gate, g1, sh, sc, wrt, xa_ref, aff_ref):
    x1 = x + gate * y
    n, d = x1.shape
    _store_packed(xa_ref, x1, n, 0, d)
    _store_packed(xa_ref, x1, n, 1, d)
    pitch = _row_pitch(d)
    for k in range(2 * (d // LANES), pitch):
        xa_ref[pl.ds(k, n, stride=pitch), :] = jnp.zeros((n, LANES), F32)
    f = _rms_mod(x1, g1, sh, sc)
    lt = _dot3_nt(wrt, f)
    e = jnp.exp(lt - jnp.max(lt, axis=0, keepdims=True))
    aff_ref[0] = e / jnp.sum(e, axis=0, keepdims=True)


def _ab_out_kernel(of_ref, ob_ref, gg_ref, nl_ref, on_ref, w_ref, x_ref, gate_ref, g1_ref, sh_ref, sc_ref, wrt_ref,
                   xa_ref, aff_ref, *, packed):
    o = of_ref[...] + ob_ref[...]
    gain = on_ref[...]
    parts = []
    for h in range(HGRN_HEADS):
        oh = o[:, h * HGRN_DK:(h + 1) * HGRN_DK]
        parts.append(oh * lax.rsqrt(jnp.mean(oh * oh, axis=-1, keepdims=True) + RMS_EPS) * gain)
    hl = (jnp.concatenate(parts, axis=-1) * _silu(gg_ref[...])).astype(BF16)
    y = _dot(hl, w_ref[0:HGRN_WIDTH, :]) + _dot(nl_ref[...], w_ref[HGRN_WIDTH:, :])
    _residual_router(_load_stream(x_ref, packed), y, gate_ref[0], g1_ref[...], sh_ref[0], sc_ref[0], wrt_ref[...],
                     xa_ref, aff_ref)


def _ssd_out_kernel(yf_ref, yb_ref, xs_ref, z_ref, dsk_ref, ng_ref, w_ref, x_ref, gate_ref, g1_ref, sh_ref, sc_ref,
                    wrt_ref, xa_ref, aff_ref, *, group_width, packed):
    y = (yf_ref[...] + yb_ref[...] + dsk_ref[...] * xs_ref[...]) * _silu(z_ref[...])
    parts = []
    for g in range(y.shape[1] // group_width):
        yg = y[:, g * group_width:(g + 1) * group_width]
        parts.append(yg * lax.rsqrt(jnp.mean(yg * yg, axis=-1, keepdims=True) + RMS_EPS))
    yn = (jnp.concatenate(parts, axis=-1) * ng_ref[...]).astype(BF16)
    _residual_router(_load_stream(x_ref, packed), _dot(yn, w_ref[...]), gate_ref[0], g1_ref[...], sh_ref[0],
                     sc_ref[0], wrt_ref[...], xa_ref, aff_ref)


def _out_common(x, gate, g1, sh, sc, wrt, *, rows, rows_per_mod, tm, packed):
    r = rows
    e, d = wrt.shape
    per = rows_per_mod // tm
    nbatch = r // rows_per_mod
    pitch = _row_pitch(d)
    mod = lambda i: (i // per, 0, 0)
    in_specs = [_stream_spec(tm, d, packed, lambda i: i), pl.BlockSpec((1, 1, d), mod),
                pl.BlockSpec((1, d), lambda i: (0, 0)),
                pl.BlockSpec((1, 1, d), mod), pl.BlockSpec((1, 1, d), mod), pl.BlockSpec((e, d), lambda i: (0, 0))]
    out_specs = [pl.BlockSpec((tm * pitch, LANES), lambda i: (i, 0)),
                 pl.BlockSpec((1, e, tm), lambda i: (i // per, 0, i % per))]
    out_shape = [jax.ShapeDtypeStruct((r * pitch, LANES), F32), jax.ShapeDtypeStruct((nbatch, e, rows_per_mod), F32)]
    args = (x, gate, g1.reshape(1, d), sh, sc, wrt)
    return in_specs, out_specs, out_shape, args


def ab_out(o_f, o_b, p, nl, onorm_g, w_out, x, gate, g1, sh, sc, wrt, *, rows_per_mod, tm, packed):
    tm = min(tm, rows_per_mod)
    r = o_f.shape[0]
    wd = HGRN_WIDTH
    row = lambda i: (i, 0)
    d = wrt.shape[1]
    c_in, c_out, c_shape, c_args = _out_common(x, gate, g1, sh, sc, wrt, rows=r, rows_per_mod=rows_per_mod, tm=tm,
                                               packed=packed)
    in_specs = [pl.BlockSpec((tm, wd), row), pl.BlockSpec((tm, wd), row),
                pl.BlockSpec((tm, wd), lambda i: (i, 4)),
                pl.BlockSpec((tm, NA_WIDTH), row),
                pl.BlockSpec((1, HGRN_DK), lambda i: (0, 0)),
                pl.BlockSpec(w_out.shape, lambda i: (0, 0))] + c_in
    return pl.pallas_call(
        functools.partial(_ab_out_kernel, packed=(tm, d) if packed else None),
        grid=(r // tm,), in_specs=in_specs, out_specs=c_out, out_shape=c_shape,
        compiler_params=_cparams("parallel"), name="ab_out",
    )(o_f, o_b, p, nl, onorm_g.reshape(1, HGRN_DK), w_out, *c_args)


def ssd_out(y_f, y_b, xa, p, dsk, ng, w_out, x, gate, g1, sh, sc, wrt, *, rows_per_mod, tm, packed):
    tm = min(tm, rows_per_mod)
    r = y_f.shape[0]
    inner = w_out.shape[0]
    row = lambda i: (i, 0)
    d = wrt.shape[1]
    c_in, c_out, c_shape, c_args = _out_common(x, gate, g1, sh, sc, wrt, rows=r, rows_per_mod=rows_per_mod, tm=tm,
                                               packed=packed)
    in_specs = [pl.BlockSpec((tm, inner), row), pl.BlockSpec((tm, inner), row),
                pl.BlockSpec((tm, inner), row),
                pl.BlockSpec((tm, inner), row),
                pl.BlockSpec((1, inner), lambda i: (0, 0)), pl.BlockSpec((1, inner), lambda i: (0, 0)),
                pl.BlockSpec(w_out.shape, lambda i: (0, 0))] + c_in
    return pl.pallas_call(
        functools.partial(_ssd_out_kernel, group_width=inner // SSD_GROUPS, packed=(tm, d) if packed else None),
        grid=(r // tm,), in_specs=in_specs, out_specs=c_out, out_shape=c_shape,
        compiler_params=_cparams("parallel"), name="ssd_out",
    )(y_f, y_b, xa, p, dsk.reshape(1, inner), ng.reshape(1, inner), w_out, *c_args)


F32_INF_BITS = 0x7F800000


def _lane_cumsum(src_ref, dst_ref, n_chunks):
    e = src_ref.shape[0]
    upper = _tri(LANES, True).astype(BF16)

    def body(j, off):
        ds = pl.ds(pl.multiple_of(j * LANES, LANES), LANES)
        c = _dot(src_ref[:, ds].astype(BF16), upper) + off
        dst_ref[:, ds] = c
        return c[:, LANES - 1:LANES]

    lax.fori_loop(0, n_chunks, body, jnp.zeros((e, 1), F32))


def _select_kernel(aff_ref, idx_ref, gate_ref, m_scr, c_scr, *, seq, cap, nph, kc, row_scale):
    e = aff_ref.shape[1]
    n_chunks = seq // LANES
    bits = pltpu.bitcast(aff_ref[0], I32)

    def bisect(_, lohi):
        lo, hi = lohi
        mid = lo + lax.shift_right_logical(hi - lo, 1)
        cnt = jnp.sum((bits >= mid).astype(F32), axis=1, keepdims=True)
        ge = cnt >= cap
        return jnp.where(ge, mid, lo), jnp.where(ge, hi, mid)

    thr, _ = lax.fori_loop(0, 31, bisect, (jnp.zeros((e, 1), I32), jnp.full((e, 1), F32_INF_BITS, I32)))
    gt = bits > thr
    eq = bits == thr
    need = cap - jnp.sum(gt.astype(F32), axis=1, keepdims=True)
    m_scr[...] = eq.astype(F32)
    _lane_cumsum(m_scr, c_scr, n_chunks)
    sel = gt | (eq & (c_scr[...] <= need))
    m_scr[...] = sel.astype(F32)
    _lane_cumsum(m_scr, c_scr, n_chunks)

    sub_p = lax.broadcasted_iota(I32, (nph, kc), 0).astype(F32)
    sub_l = lax.broadcasted_iota(I32, (LANES, kc), 0).astype(F32)
    lane_t = lax.broadcasted_iota(I32, (1, kc), 1)
    inv = 1.0 / LANES
    for ex in range(e):
        def body(c, acc):
            ds = pl.ds(pl.multiple_of(c * kc, LANES), kc)
            pos = c_scr[ex:ex + 1, ds] - 1.0
            phi = jnp.floor(pos * inv)
            plo = pos - LANES * phi
            hit = (phi == sub_p) & (m_scr[ex:ex + 1, ds] > 0.0)
            t = (lane_t + c * kc).astype(F32)
            th = jnp.floor(t * inv)
            tl = t - LANES * th
            a1, a2, a3 = _split3(aff_ref[0, ex:ex + 1, ds])
            rows = [th, tl, a1.astype(F32), a2.astype(F32), a3.astype(F32)]
            lhs = jnp.concatenate([jnp.where(hit, r, 0.0) for r in rows], axis=0).astype(BF16)
            onehot = (plo == sub_l).astype(BF16)
            return acc + _dot_nt(lhs, onehot)

        acc = lax.fori_loop(0, seq // kc, body, jnp.zeros((5 * nph, LANES), F32))
        token = (acc[0:nph] * LANES + acc[nph:2 * nph]).astype(I32)
        idx_ref[0, ex] = (pl.program_id(0) * seq + token) * row_scale
        gate_ref[0, ex] = acc[2 * nph:3 * nph] + acc[3 * nph:4 * nph] + acc[4 * nph:5 * nph]


def expert_select(aff_t, cap, row_scale):
    bsz, e, seq = aff_t.shape
    nph = -(-cap // LANES)
    nph = -(-nph // SUBLANES) * SUBLANES
    kc = min(2048, seq)
    idx, gate = pl.pallas_call(
        functools.partial(_select_kernel, seq=seq, cap=cap, nph=nph, kc=kc, row_scale=row_scale),
        grid=(bsz,),
        in_specs=[pl.BlockSpec((1, e, seq), lambda b: (b, 0, 0))],
        out_specs=[pl.BlockSpec((1, e, nph, LANES), lambda b: (b, 0, 0, 0)),
                   pl.BlockSpec((1, e, nph, LANES), lambda b: (b, 0, 0, 0))],
        out_shape=[jax.ShapeDtypeStruct((bsz, e, nph, LANES), I32),
                   jax.ShapeDtypeStruct((bsz, e, nph, LANES), F32)],
        scratch_shapes=[pltpu.VMEM((e, seq), F32), pltpu.VMEM((e, seq), F32)],
        compiler_params=_cparams("parallel"),
        name="expert_select",
    )(aff_t)
    return idx.reshape(bsz, e, nph * LANES)[:, :, :cap], gate.reshape(bsz, e, nph * LANES)[:, :, :cap]


FFN_SLOTS = 3
FFN_TILE = 512


def _ffn_kernel(idx_ref, gate_ref, g1_ref, sh_ref, sc_ref, g5_ref, w1_ref, w3_ref, w2_ref, xa_in,
                xa_hbm, buf, gsem, ssem, *, tr, nt, d):
    del xa_in
    nx = d // LANES
    pitch = _row_pitch(d)
    grp = min(8, tr)
    j = pl.program_id(2)
    step = (pl.program_id(0) * pl.num_programs(1) + pl.program_id(1)) * nt + j
    slot = step % FFN_SLOTS
    nslot = (step + 1) % FFN_SLOTS

    def start_gathers(tile, sl):
        def body(i, c):
            for k in range(grp):
                src = idx_ref[0, 0, tile * tr + i * grp + k]
                dst = i * (grp * pitch) + k * pitch
                pltpu.make_async_copy(xa_hbm.at[pl.ds(src, 2 * nx)], buf.at[sl, pl.ds(dst, 2 * nx)],
                                      gsem.at[sl]).start()
            return c
        lax.fori_loop(0, tr // grp, body, 0)

    def start_scatters(tile, sl):
        def body(i, c):
            for k in range(grp):
                dst = idx_ref[0, 0, tile * tr + i * grp + k] + nx
                src = i * (grp * pitch) + k * pitch + nx
                pltpu.make_async_copy(buf.at[sl, pl.ds(src, nx)], xa_hbm.at[pl.ds(dst, nx)], ssem.at[sl]).start()
            return c
        lax.fori_loop(0, tr // grp, body, 0)

    def wait_gathers(sl):
        n = tr * 2 * nx
        pltpu.make_async_copy(xa_hbm.at[pl.ds(0, n)], buf.at[sl, pl.ds(0, n)], gsem.at[sl]).wait()

    def wait_scatters(sl):
        n = tr * nx
        pltpu.make_async_copy(buf.at[sl, pl.ds(0, n)], xa_hbm.at[pl.ds(0, n)], ssem.at[sl]).wait()

    @pl.when(j == 0)
    def _():
        start_gathers(0, slot)

    @pl.when(j + 1 < nt)
    def _():
        @pl.when(j >= FFN_SLOTS - 1)
        def _():
            wait_scatters(nslot)

        start_gathers(j + 1, nslot)

    wait_gathers(slot)
    rows = buf.at[slot]
    f = _rms_mod(_load_packed(rows, tr, 0, d), g1_ref[...], sh_ref[0], sc_ref[0]).astype(BF16)
    hid = (_silu(_dot(f, w1_ref[0])) * _dot(f, w3_ref[0])).astype(BF16)
    y = _dot(hid, w2_ref[0])
    eye = lax.broadcasted_iota(I32, (tr, tr), 0) == lax.broadcasted_iota(I32, (tr, tr), 1)
    gcol = jnp.sum(jnp.where(eye, gate_ref[0], 0.0), axis=1, keepdims=True)
    _store_packed(rows, _load_packed(rows, tr, 1, d) + (g5_ref[0] * gcol) * y, tr, 1, d)
    start_scatters(j, slot)

    @pl.when(j == nt - 1)
    def _():
        for back in range(min(FFN_SLOTS, nt)):
            wait_scatters((step - back) % FFN_SLOTS)


def expert_ffn_combine(xa, idx, gate, g1, sh, sc, g5, w1, w3, w2):
    bsz, e, cap = idx.shape
    d, ff = w1.shape[1], w1.shape[2]
    tr = min(FFN_TILE, cap)
    nt = cap // tr
    mod = lambda b, ex, j: (b, 0, 0)
    return pl.pallas_call(
        functools.partial(_ffn_kernel, tr=tr, nt=nt, d=d),
        grid=(bsz, e, nt),
        in_specs=[pl.BlockSpec((1, 1, cap), lambda b, ex, j: (b * e + ex, 0, 0), memory_space=pltpu.SMEM),
                  pl.BlockSpec((1, 1, tr), lambda b, ex, j: ((b * e + ex) * nt + j, 0, 0)),
                  pl.BlockSpec((1, d), lambda b, ex, j: (0, 0)),
                  pl.BlockSpec((1, 1, d), mod), pl.BlockSpec((1, 1, d), mod), pl.BlockSpec((1, 1, d), mod),
                  pl.BlockSpec((1, d, ff), lambda b, ex, j: (ex, 0, 0)),
                  pl.BlockSpec((1, d, ff), lambda b, ex, j: (ex, 0, 0)),
                  pl.BlockSpec((1, ff, d), lambda b, ex, j: (ex, 0, 0)),
                  pl.BlockSpec(memory_space=pl.ANY)],
        out_specs=pl.BlockSpec(memory_space=pl.ANY),
        out_shape=jax.ShapeDtypeStruct(xa.shape, F32),
        scratch_shapes=[pltpu.VMEM((FFN_SLOTS, tr * _row_pitch(d), LANES), F32),
                        pltpu.SemaphoreType.DMA((FFN_SLOTS,)), pltpu.SemaphoreType.DMA((FFN_SLOTS,))],
        input_output_aliases={9: 0},
        compiler_params=_cparams("arbitrary", "arbitrary", "arbitrary"),
        name="expert_ffn_combine",
    )(idx.reshape(bsz * e, 1, cap), gate.reshape(bsz * e * nt, 1, tr), g1.reshape(1, d), sh, sc, g5,
      w1, w3, w2, xa)


CONV_HALO = 8


def _ssd_in_kernel(cur_ref, prev_ref, next_ref, g_ref, sh_ref, sc_ref, w_ref, cw_ref, cb_ref, dtb_ref,
                   z_ref, xa_ref, dtv_ref, ext_ref, *, tm, per, inner, cdim, packed):
    i = pl.program_id(0)
    first = (i % per) == 0
    last = (i % per) == per - 1
    halo = (CONV_HALO, packed[1]) if packed else None

    def normed(ref, pk):
        return _rms_mod(_load_stream(ref, pk), g_ref[...], sh_ref[0], sc_ref[0]).astype(BF16)

    h = normed(cur_ref, packed)
    hp = normed(prev_ref, halo)
    hn = normed(next_ref, halo)
    z_ref[...] = _dot(h, w_ref[:, 0:inner])
    v = _dot(h, w_ref[:, inner + cdim:inner + cdim + LANES]) + dtb_ref[...]
    dtv_ref[...] = jnp.maximum(v, 0.0) + jnp.log1p(jnp.exp(-jnp.abs(v)))
    half = cdim // 2
    lo = CONV_HALO - SSD_CONV // 2
    for j in range(2):
        cols = slice(inner + j * half, inner + (j + 1) * half)
        out = slice(j * half, (j + 1) * half)
        ext_ref[0:CONV_HALO, :] = jnp.where(first, 0.0, _dot(hp, w_ref[:, cols]))
        ext_ref[CONV_HALO:CONV_HALO + tm, :] = _dot(h, w_ref[:, cols])
        ext_ref[CONV_HALO + tm:2 * CONV_HALO + tm, :] = jnp.where(last, 0.0, _dot(hn, w_ref[:, cols]))
        acc = cb_ref[:, out] + cw_ref[0:1, out] * ext_ref[pl.ds(lo, tm), :]
        for k in range(1, SSD_CONV):
            acc = acc + cw_ref[k:k + 1, out] * ext_ref[pl.ds(lo + k, tm), :]
        xa_ref[:, out] = _silu(acc)


def ssd_in(x, g, shift, scale, w, conv_w, conv_b, dt_bias, *, rows, rows_per_seq, tm, inner, packed):
    r = rows
    d = w.shape[0]
    cdim = conv_w.shape[1]
    tm = min(tm, rows_per_seq)
    per = rows_per_seq // tm
    hb = tm // CONV_HALO
    nhb = r // CONV_HALO
    const = lambda i: (0, 0)
    mod = lambda i: (i // per, 0, 0)
    return pl.pallas_call(
        functools.partial(_ssd_in_kernel, tm=tm, per=per, inner=inner, cdim=cdim,
                          packed=(tm, d) if packed else None),
        grid=(r // tm,),
        in_specs=[_stream_spec(tm, d, packed, lambda i: i),
                  _stream_spec(CONV_HALO, d, packed, lambda i: jnp.maximum(i * hb - 1, 0)),
                  _stream_spec(CONV_HALO, d, packed, lambda i: jnp.minimum((i + 1) * hb, nhb - 1)),
                  pl.BlockSpec((1, d), const), pl.BlockSpec((1, 1, d), mod), pl.BlockSpec((1, 1, d), mod),
                  pl.BlockSpec(w.shape, const), pl.BlockSpec((SSD_CONV, cdim), const),
                  pl.BlockSpec((1, cdim), const), pl.BlockSpec((1, LANES), const)],
        out_specs=[pl.BlockSpec((tm, inner), lambda i: (i, 0)), pl.BlockSpec((tm, cdim), lambda i: (i, 0)),
                   pl.BlockSpec((tm, LANES), lambda i: (i, 0))],
        out_shape=[jax.ShapeDtypeStruct((r, inner), F32), jax.ShapeDtypeStruct((r, cdim), F32),
                   jax.ShapeDtypeStruct((r, LANES), F32)],
        scratch_shapes=[pltpu.VMEM((tm + 2 * CONV_HALO, cdim // 2), F32)],
        compiler_params=_cparams("parallel"),
        name="ssd_in",
    )(x, x, x, g.reshape(1, d), shift, scale, w, conv_w, conv_b.reshape(1, cdim), dt_bias)


SSD_HPG = 4


def _ssd_block(x, bm, cm, dtv, a_row, hs, lane0, upper, need_y=True):
    t = x.shape[0]
    hp = x.shape[1]
    pdim = hp // SSD_HPG
    ch = SSD_CHUNK
    nc = t // ch
    expand = (lax.broadcasted_iota(I32, (LANES, hp), 0) ==
              lane0 + lax.broadcasted_iota(I32, (LANES, hp), 1) // pdim).astype(BF16)
    dtx = sum(_dot(p, expand) for p in _split3(dtv))
    a_x = sum(_dot(p, expand) for p in _split3(jnp.broadcast_to(a_row, (SUBLANES, LANES))))[0:1]
    dtax = dtx * a_x
    c3 = _dot(_block_tri(t, ch, upper).astype(BF16), jnp.concatenate(_split3(dtax), axis=1))
    acx = c3[:, 0:hp] + c3[:, hp:2 * hp] + c3[:, 2 * hp:3 * hp]
    r = lax.broadcasted_iota(I32, (ch, hp), 0)
    s = lax.broadcasted_iota(I32, (ch, hp), 1) % ch
    src_le = (s <= r) if upper else (s >= r)
    ident = s == r
    causal = (r <= s) if upper else (r >= s)
    same_head = (lax.broadcasted_iota(I32, (hp, hp), 0) // pdim) == (lax.broadcasted_iota(I32, (hp, hp), 1) // pdim)
    ydiag, decay_out, alasts, upds = [], [], [], []
    for c in range(nc):
        rs = slice(c * ch, (c + 1) * ch)
        acx_c, dtx_c, x_c = acx[rs], dtx[rs], x[rs]
        bm_c = bm[rs].astype(BF16)
        alast = acx_c[0:1] if upper else acx_c[ch - 1:ch]
        if need_y:
            arx = jnp.sum(jnp.where(src_le, dtax[rs], 0.0), axis=0, keepdims=True)
            dtr = jnp.sum(jnp.where(ident, dtx_c, 0.0), axis=0, keepdims=True)
            dec = jnp.where(causal, jnp.exp(jnp.minimum(acx_c - arx, 0.0)), 0.0)
            cbx = _dot_nt(cm[rs].astype(BF16), jnp.concatenate([bm_c] * SSD_HPG, axis=0))
            gx = (cbx * dec * dtr).astype(BF16)
            xbd = jnp.where(same_head, jnp.concatenate([x_c] * SSD_HPG, axis=0), 0.0).astype(BF16)
            ydiag.append(_dot(gx, xbd))
            decay_out.append(jnp.exp(acx_c))
        alasts.append(alast)
        upds.append(_dot_tn(bm_c, (x_c * (jnp.exp(alast - acx_c) * dtx_c)).astype(BF16)))
    ys = [None] * nc
    for c in (reversed(range(nc)) if upper else range(nc)):
        if need_y:
            ys[c] = ydiag[c] + _dot(cm[c * ch:(c + 1) * ch].astype(BF16), hs.astype(BF16)) * decay_out[c]
        hs = hs * jnp.exp(alasts[c]) + upds[c]
    return (jnp.concatenate(ys, axis=0) if need_y else None), hs


def _ssd_kernel(xf_ref, bf_ref, cf_ref, df_ref, xb_ref, bb_ref, cb_ref, db_ref,
                xc_ref, bc_ref, cc_ref, dc_ref, alog_ref, yf_ref, yb_ref, hf_ref, hb_ref):
    g = pl.program_id(0) % SSD_GROUPS
    a_row = -jnp.exp(alog_ref[...])
    lane_f = g * SSD_HPG
    lane_b = SSD_GROUPS * SSD_HPG + g * SSD_HPG

    def run(x_ref, b_ref, c_ref, d_ref, y_ref, hs, lane0, upper):
        n = x_ref.shape[0]
        sub = min(SCAN_SUB, n)
        order = range(n // sub)
        for sb in (reversed(order) if upper else order):
            rs = slice(sb * sub, (sb + 1) * sub)
            y, hs = _ssd_block(x_ref[rs, :], b_ref[rs, :], c_ref[rs, :], d_ref[rs, :], a_row, hs, lane0, upper,
                               need_y=y_ref is not None)
            if y_ref is not None:
                y_ref[rs, :] = y
        return hs

    @pl.when(pl.program_id(1) == 0)
    def _():
        z0 = jnp.zeros(hf_ref.shape, F32)
        hf_ref[...] = run(xc_ref, bc_ref, cc_ref, dc_ref, None, z0, lane_f, False)
        hb_ref[...] = run(xc_ref, bc_ref, cc_ref, dc_ref, None, z0, lane_b, True)

    hf_ref[...] = run(xf_ref, bf_ref, cf_ref, df_ref, yf_ref, hf_ref[...], lane_f, False)
    hb_ref[...] = run(xb_ref, bb_ref, cb_ref, db_ref, yb_ref, hb_ref[...], lane_b, True)


def ssd_scan(xa, dtv, xa_c, dtv_c, a_log_row, *, bsz, seq, lc, tb, inner):
    g = SSD_GROUPS
    n = SSD_STATE
    xw = inner // g
    nb = seq // tb
    cb0 = inner // n
    cc0 = cb0 + g

    def blk(i, rev):
        return nb - 1 - i if rev else i

    def lat(width, col, rev):
        return pl.BlockSpec((tb, width), lambda bg, i: ((bg // g) * nb + blk(i, rev), col + bg % g))

    def lat_dt(rev):
        return pl.BlockSpec((tb, LANES), lambda bg, i: ((bg // g) * nb + blk(i, rev), 0))

    def ctx(width, col):
        return pl.BlockSpec((lc, width), lambda bg, i: (bg // g, col + bg % g))

    def out(rev):
        return pl.BlockSpec((tb, xw), lambda bg, i: ((bg // g) * nb + blk(i, rev), bg % g))

    return pl.pallas_call(
        _ssd_kernel,
        grid=(bsz * g, nb),
        in_specs=[lat(xw, 0, False), lat(n, cb0, False), lat(n, cc0, False), lat_dt(False),
                  lat(xw, 0, True), lat(n, cb0, True), lat(n, cc0, True), lat_dt(True),
                  ctx(xw, 0), ctx(n, cb0), ctx(n, cc0), pl.BlockSpec((lc, LANES), lambda bg, i: (bg // g, 0)),
                  pl.BlockSpec((1, LANES), lambda bg, i: (0, 0))],
        out_specs=[out(False), out(True)],
        out_shape=[jax.ShapeDtypeStruct((bsz * seq, inner), F32), jax.ShapeDtypeStruct((bsz * seq, inner), F32)],
        scratch_shapes=[pltpu.VMEM((n, xw), F32), pltpu.VMEM((n, xw), F32)],
        compiler_params=_cparams("parallel", "arbitrary", flags=SCAN_FLAGS),
        name="ssd_scan",
    )(xa, xa, xa, dtv, xa, xa, xa, dtv, xa_c, xa_c, xa_c, dtv_c, a_log_row)


def _final_norm_kernel(x_ref, g_ref, o_ref, *, packed):
    x = _load_stream(x_ref, packed)
    o_ref[...] = x * lax.rsqrt(jnp.mean(x * x, axis=-1, keepdims=True) + RMS_EPS) * g_ref[...]


def final_norm(x, g, *, rows, tm, packed):
    r = rows
    d = g.shape[0]
    return pl.pallas_call(
        functools.partial(_final_norm_kernel, packed=(tm, d) if packed else None),
        grid=(r // tm,),
        in_specs=[_stream_spec(tm, d, packed, lambda i: i), pl.BlockSpec((1, d), lambda i: (0, 0))],
        out_specs=pl.BlockSpec((tm, d), lambda i: (i, 0)),
        out_shape=jax.ShapeDtypeStruct((r, d), F32),
        compiler_params=_cparams("parallel"),
        name="final_norm",
    )(x, g.reshape(1, d))


def _moe(xa, aff, g1, sh, sc, g5, w1, w3, w2, *, seq):
    cap = max(1, CAPACITY_FACTOR * seq // N_EXPERTS)
    idx, gate = expert_select(aff, cap, _row_pitch(w1.shape[1]))
    return expert_ffn_combine(xa, idx, gate, g1, sh, sc, g5, w1, w3, w2)


def kernel(x, c, ctx, c_ctx, ada_w, ada_b, norm_g, final_g, ab_w_in, ab_w_out, hgrn_lb_logits, hgrn_onorm_g, na_rpb, ssd_w_in, ssd_conv_w, ssd_conv_b, ssd_a_log, ssd_dt_bias, ssd_d, ssd_norm_g, ssd_w_out, moe_router, moe_w1, moe_w3, moe_w2):
    bsz, seq, d = x.shape
    lc = ctx.shape[1]
    depth = ada_w.shape[0]
    assert depth == 2, "context outputs of an SSD layer are not implemented (only needed when a layer follows it)"
    mods = modulation_table(c, c_ctx, ada_w, ada_b)
    lb_all = jnp.cumsum(jax.nn.softmax(hgrn_lb_logits.astype(F32), axis=1), axis=1)
    xl = x.reshape(bsz * seq, d)
    xc = ctx.reshape(bsz * lc, d)
    packed = False
    hg = 5 * HGRN_WIDTH
    for l in range(depth):
        need_ctx = l < depth - 1
        k = l // 2
        m = mods[l]
        ml = [m[:bsz, j * d:(j + 1) * d].reshape(bsz, 1, d) for j in range(6)]
        mc = [jnp.broadcast_to(m[bsz, j * d:(j + 1) * d].reshape(1, 1, d), (bsz, 1, d)) for j in range(6)]
        wrt = moe_router[l].T
        w1, w3, w2 = (w[l].astype(BF16) for w in (moe_w1, moe_w3, moe_w2))
        g0, g1 = norm_g[l, 0], norm_g[l, 1]
        if l % 2 == 0:
            w_in = ab_w_in[k].astype(BF16)
            w_out = ab_w_out[k].astype(BF16)
            p_lat = norm_mod_project(xl, g0, ml[0], ml[1], w_in, rows=bsz * seq, rows_per_mod=seq, tm=512, tn=1024,
                                     packed=packed)
            p_ctx = norm_mod_project(xc, g0, mc[0], mc[1], w_in, rows=bsz * lc, rows_per_mod=lc, tm=512, tn=1024,
                                     packed=packed)
            o_f, o_b, oc_f, oc_b = hgrn_scan(p_lat, p_ctx, lb_all[:, k], bsz=bsz, seq=seq, lc=lc, tb=min(512, seq))
            nl = neighborhood_attention(p_lat, p_ctx, na_bias_table(na_rpb[k]), bsz=bsz, seq=seq, lc=lc, col0=hg)
            xa, aff = ab_out(o_f, o_b, p_lat, nl, hgrn_onorm_g[k], w_out, xl, ml[2], g1, ml[3], ml[4], wrt,
                             rows_per_mod=seq, tm=512, packed=packed)
            if need_ctx:
                nc = context_attention(p_ctx, bsz=bsz, lc=lc, col0=hg)
                ca, caff = ab_out(oc_f, oc_b, p_ctx, nc, hgrn_onorm_g[k], w_out, xc, mc[2], g1, mc[3], mc[4],
                                  wrt, rows_per_mod=lc, tm=512, packed=packed)
        else:
            inner = ssd_w_out.shape[1]
            heads2 = ssd_dt_bias.shape[1] * ssd_dt_bias.shape[2]
            wz = ssd_w_in[k]
            pad = jnp.zeros((d, LANES - heads2), F32)
            w_in = jnp.concatenate([wz, pad], axis=1).astype(BF16)
            w_out = ssd_w_out[k].astype(BF16)
            dtb = jnp.zeros((1, LANES), F32).at[0, :heads2].set(ssd_dt_bias[k].reshape(-1))
            alog = jnp.zeros((1, LANES), F32).at[0, :heads2].set(ssd_a_log[k].reshape(-1))
            proj = functools.partial(ssd_in, g=g0, w=w_in, conv_w=ssd_conv_w[k], conv_b=ssd_conv_b[k], dt_bias=dtb,
                                     tm=256, inner=inner, packed=packed)
            z, xs, dtv = proj(xl, shift=ml[0], scale=ml[1], rows=bsz * seq, rows_per_seq=seq)
            _, xs_c, dtv_c = proj(xc, shift=mc[0], scale=mc[1], rows=bsz * lc, rows_per_seq=lc)
            y_f, y_b = ssd_scan(xs, dtv, xs_c, dtv_c, alog, bsz=bsz, seq=seq, lc=lc, tb=min(512, seq), inner=inner)
            dsk = jnp.repeat(ssd_d[k].astype(F32), SSD_HEADDIM)
            xa, aff = ssd_out(y_f, y_b, xs, z, dsk, ssd_norm_g[k], w_out, xl, ml[2], g1, ml[3], ml[4], wrt,
                              rows_per_mod=seq, tm=256, packed=packed)
        xl = _moe(xa, aff, g1, ml[3], ml[4], ml[5], w1, w3, w2, seq=seq)
        if need_ctx:
            xc = _moe(ca, caff, g1, mc[3], mc[4], mc[5], w1, w3, w2, seq=lc)
        packed = True
    return final_norm(xl, final_g, rows=bsz * seq, tm=512, packed=packed).reshape(bsz, seq, d)
```

```python
import functools
import math

import jax
import jax.numpy as jnp
import numpy as np
from jax import lax
from jax.experimental import pallas as pl
from jax.experimental.pallas import tpu as pltpu

F32 = jnp.float32
BF16 = jnp.bfloat16
I32 = jnp.int32

GRID_W = 64
HGRN_HEADS = 4
HGRN_DK = 128
HGRN_WIDTH = HGRN_HEADS * HGRN_DK
GLA_CHUNK = 32
NA_HEADS = 8
NA_DH = 64
NA_WIDTH = NA_HEADS * NA_DH
NA_KR = 8
NA_KC = 16
SSD_HEADDIM = 64
SSD_GROUPS = 8
SSD_STATE = 128
SSD_CONV = 5
SSD_CHUNK = 64
N_EXPERTS = 16
CAPACITY_FACTOR = 2
RMS_EPS = 1e-6
NEG_INF = -1e30

LANES = 128
SUBLANES = 8
VMEM_LIMIT_BYTES = 56 * 1024 * 1024


def _cparams(*sem, flags=None):
    return pltpu.CompilerParams(dimension_semantics=sem, vmem_limit_bytes=VMEM_LIMIT_BYTES, flags=flags)


SCAN_FLAGS = None


def _sigmoid(x):
    return 1.0 / (1.0 + jnp.exp(-x))


def _silu(x):
    return x * _sigmoid(x)


def _split3(x):
    a = x.astype(BF16)
    r = x - a.astype(F32)
    b = r.astype(BF16)
    c = (r - b.astype(F32)).astype(BF16)
    return a, b, c


def _dot(a, b):
    return jnp.dot(a, b, preferred_element_type=F32)


def _dot_nt(a, b):
    return lax.dot_general(a, b, (((1,), (1,)), ((), ())), preferred_element_type=F32)


def _dot_tn(a, b):
    return lax.dot_general(a, b, (((0,), (0,)), ((), ())), preferred_element_type=F32)


def _dot_exact_lhs(m01, x):
    m = m01.astype(BF16)
    a, b, c = _split3(x)
    return _dot(m, a) + _dot(m, b) + _dot(m, c)


def _dot3(a, b):
    a1 = a.astype(BF16)
    a2 = (a - a1.astype(F32)).astype(BF16)
    b1 = b.astype(BF16)
    b2 = (b - b1.astype(F32)).astype(BF16)
    return _dot(a1, b1) + _dot(a1, b2) + _dot(a2, b1)


def _dot3_nt(a, b):
    a1 = a.astype(BF16)
    a2 = (a - a1.astype(F32)).astype(BF16)
    b1 = b.astype(BF16)
    b2 = (b - b1.astype(F32)).astype(BF16)
    return _dot_nt(a1, b1) + _dot_nt(a1, b2) + _dot_nt(a2, b1)


def _rms_mod(x, g, shift, scale):
    y = x * lax.rsqrt(jnp.mean(x * x, axis=-1, keepdims=True) + RMS_EPS) * g
    return y * (1.0 + scale) + shift


def _row_pitch(d):
    rows = 2 * (d // LANES)
    return rows + (4 - rows % SUBLANES) % SUBLANES


def _load_packed(ref, n, part, d):
    nx = d // LANES
    pitch = _row_pitch(d)
    return jnp.concatenate([ref[pl.ds(part * nx + k, n, stride=pitch), :] for k in range(nx)], axis=1)


def _store_packed(ref, val, n, part, d):
    nx = d // LANES
    pitch = _row_pitch(d)
    for k in range(nx):
        ref[pl.ds(part * nx + k, n, stride=pitch), :] = val[:, k * LANES:(k + 1) * LANES]


def _load_stream(x_ref, packed):
    if not packed:
        return x_ref[...]
    n, d = packed
    return _load_packed(x_ref, n, 1, d)


def _stream_spec(tm, d, packed, index_map_row):
    if packed:
        return pl.BlockSpec((tm * _row_pitch(d), LANES), lambda *a: (index_map_row(*a), 0))
    return pl.BlockSpec((tm, d), lambda *a: (index_map_row(*a), 0))


def _mod_kernel(s_ref, w_ref, b_ref, o_ref):
    s = _silu(s_ref[...])
    o_ref[0] = _dot3(s, w_ref[0]) + b_ref[0]


def modulation_table(c, c_ctx, ada_w, ada_b):
    depth, d, n = ada_w.shape
    bsz = c.shape[0]
    s = jnp.zeros((SUBLANES, d), F32).at[:bsz].set(c).at[bsz].set(c_ctx)
    tn = 1024
    return pl.pallas_call(
        _mod_kernel,
        grid=(depth, n // tn),
        in_specs=[pl.BlockSpec((SUBLANES, d), lambda l, j: (0, 0)),
                  pl.BlockSpec((1, d, tn), lambda l, j: (l, 0, j)),
                  pl.BlockSpec((1, 1, tn), lambda l, j: (l, 0, j))],
        out_specs=pl.BlockSpec((1, SUBLANES, tn), lambda l, j: (l, 0, j)),
        out_shape=jax.ShapeDtypeStruct((depth, SUBLANES, n), F32),
        compiler_params=_cparams("parallel", "parallel"),
        name="mod_table",
    )(s, ada_w, ada_b.reshape(depth, 1, n))


def _proj_kernel(x_ref, g_ref, sh_ref, sc_ref, w_ref, o_ref, *, tn, packed):
    h = _rms_mod(_load_stream(x_ref, packed), g_ref[...], sh_ref[0], sc_ref[0]).astype(BF16)
    for j in range(w_ref.shape[1] // tn):
        o_ref[:, j * tn:(j + 1) * tn] = _dot(h, w_ref[:, j * tn:(j + 1) * tn])


def norm_mod_project(x, g, shift, scale, w, *, rows, rows_per_mod, tm, tn, packed):
    r = rows
    d, n = w.shape
    tm = min(tm, rows_per_mod)
    per = rows_per_mod // tm
    return pl.pallas_call(
        functools.partial(_proj_kernel, tn=tn, packed=(tm, d) if packed else None),
        grid=(r // tm,),
        in_specs=[_stream_spec(tm, d, packed, lambda i: i),
                  pl.BlockSpec((1, d), lambda i: (0, 0)),
                  pl.BlockSpec((1, 1, d), lambda i: (i // per, 0, 0)),
                  pl.BlockSpec((1, 1, d), lambda i: (i // per, 0, 0)),
                  pl.BlockSpec((d, n), lambda i: (0, 0))],
        out_specs=pl.BlockSpec((tm, n), lambda i: (i, 0)),
        out_shape=jax.ShapeDtypeStruct((r, n), F32),
        compiler_params=_cparams("parallel"),
        name="norm_mod_project",
    )(x, g.reshape(1, d), shift, scale, w)


def _tri(n, upper):
    r = lax.broadcasted_iota(I32, (n, n), 0)
    c = lax.broadcasted_iota(I32, (n, n), 1)
    return (r <= c) if upper else (r >= c)


SCAN_SUB = 256


def _block_tri(n, chunk, upper):
    r = lax.broadcasted_iota(I32, (n, n), 0)
    c = lax.broadcasted_iota(I32, (n, n), 1)
    same = (r // chunk) == (c // chunk)
    return same & ((r <= c) if upper else (r >= c))


def _hgrn_block(q, z, v, lb, st, upper):
    n = q.shape[0]
    ch = GLA_CHUNK
    nc = n // ch
    mask = _block_tri(n, ch, upper)
    qs = _silu(q)
    f = lb + (1.0 - lb) * _sigmoid(z)
    g = jnp.log(f)
    k = 1.0 - f
    c3 = _dot(mask.astype(BF16), jnp.concatenate(_split3(g), axis=1))
    dk = g.shape[1]
    cum = c3[:, 0:dk] + c3[:, dk:2 * dk] + c3[:, 2 * dk:3 * dk]
    cls = [cum[c * ch:c * ch + 1] if upper else cum[(c + 1) * ch - 1:(c + 1) * ch] for c in range(nc)]
    clb = jnp.concatenate([jnp.broadcast_to(cl, (ch, dk)) for cl in cls], axis=0)
    q_in = (qs * jnp.exp(cum)).astype(BF16)
    k_in = (k * jnp.exp(-cum)).astype(BF16)
    k_out = (k * jnp.exp(clb - cum)).astype(BF16)
    vb = v.astype(BF16)
    attn = jnp.where(mask, _dot_nt(q_in, k_in), 0.0)
    intra = _dot(attn.astype(BF16), vb)
    dv = v.shape[1]
    vt = v.T
    chunk_of_col = lax.broadcasted_iota(I32, (1, n), 1) // ch
    vexp_t = jnp.concatenate([jnp.where(chunk_of_col == c, vt, 0.0) for c in range(nc)], axis=0).astype(BF16)
    upd = _dot(vexp_t, k_out)
    sts = [None] * nc
    for c in (reversed(range(nc)) if upper else range(nc)):
        sts[c] = st
        st = st * jnp.exp(cls[c]) + upd[c * dv:(c + 1) * dv]
    chunk_of_row = lax.broadcasted_iota(I32, (n, 1), 0) // ch
    q_exp = jnp.concatenate([jnp.where(chunk_of_row == c, q_in, jnp.zeros_like(q_in)) for c in range(nc)], axis=1)
    st_cat = jnp.concatenate(sts, axis=1).astype(BF16)
    return intra + _dot_nt(q_exp, st_cat), st


def _hgrn_kernel(qf_ref, zf_ref, vf_ref, qb_ref, zb_ref, vb_ref,
                 qc_ref, zfc_ref, zbc_ref, vc_ref, lb_ref,
                 of_ref, ob_ref, ocf_ref, ocb_ref, stf_ref, stb_ref):
    lbf = lb_ref[0:1]
    lbb = lb_ref[1:2]

    def run(q_ref, z_ref, v_ref, lb, o_ref, st, upper):
        n = q_ref.shape[0]
        sub = min(SCAN_SUB, n)
        order = range(n // sub)
        for sb in (reversed(order) if upper else order):
            rs = slice(sb * sub, (sb + 1) * sub)
            o_ref[rs, :], st = _hgrn_block(q_ref[rs, :], z_ref[rs, :], v_ref[rs, :], lb, st, upper)
        return st

    @pl.when(pl.program_id(1) == 0)
    def _():
        z0 = jnp.zeros((HGRN_DK, HGRN_DK), F32)
        stf_ref[...] = run(qc_ref, zfc_ref, vc_ref, lbf, ocf_ref, z0, False)
        stb_ref[...] = run(qc_ref, zbc_ref, vc_ref, lbb, ocb_ref, z0, True)

    stf_ref[...] = run(qf_ref, zf_ref, vf_ref, lbf, of_ref, stf_ref[...], False)
    stb_ref[...] = run(qb_ref, zb_ref, vb_ref, lbb, ob_ref, stb_ref[...], True)


def hgrn_scan(p_lat, p_ctx, lb, *, bsz, seq, lc, tb):
    h = HGRN_HEADS
    dk = HGRN_DK
    nb = seq // tb
    cq, czf, czb, cv = 0, h, 2 * h, 3 * h

    def lat(col, rev):
        if rev:
            return pl.BlockSpec((tb, dk), lambda bh, i: ((bh // h) * nb + nb - 1 - i, col + bh % h))
        return pl.BlockSpec((tb, dk), lambda bh, i: ((bh // h) * nb + i, col + bh % h))

    def ctx(col):
        return pl.BlockSpec((lc, dk), lambda bh, i: (bh // h, col + bh % h))

    def out_lat(rev):
        if rev:
            return pl.BlockSpec((tb, dk), lambda bh, i: ((bh // h) * nb + nb - 1 - i, bh % h))
        return pl.BlockSpec((tb, dk), lambda bh, i: ((bh // h) * nb + i, bh % h))

    out_ctx = pl.BlockSpec((lc, dk), lambda bh, i: (bh // h, bh % h))
    w = h * dk
    return pl.pallas_call(
        _hgrn_kernel,
        grid=(bsz * h, nb),
        in_specs=[lat(cq, False), lat(czf, False), lat(cv, False),
                  lat(cq, True), lat(czb, True), lat(cv, True),
                  ctx(cq), ctx(czf), ctx(czb), ctx(cv),
                  pl.BlockSpec((2, dk), lambda bh, i: (0, bh % h))],
        out_specs=[out_lat(False), out_lat(True), out_ctx, out_ctx],
        out_shape=[jax.ShapeDtypeStruct((bsz * seq, w), F32), jax.ShapeDtypeStruct((bsz * seq, w), F32),
                   jax.ShapeDtypeStruct((bsz * lc, w), F32), jax.ShapeDtypeStruct((bsz * lc, w), F32)],
        scratch_shapes=[pltpu.VMEM((dk, dk), F32), pltpu.VMEM((dk, dk), F32)],
        compiler_params=_cparams("parallel", "arbitrary", flags=SCAN_FLAGS),
        name="hgrn_scan",
    )(p_lat, p_lat, p_lat, p_lat, p_lat, p_lat, p_ctx, p_ctx, p_ctx, p_ctx, lb)


NA_RB = 8


def na_bias_table(rpb):
    w = GRID_W
    qc = np.arange(w)
    win0 = np.clip(qc - NA_KC // 2, 0, w - NA_KC)
    kc = np.arange(w)
    ok = (kc[None, :] >= win0[:, None]) & (kc[None, :] < win0[:, None] + NA_KC)
    dc = np.clip(kc[None, :] - qc[:, None] + NA_KC - 1, 0, 2 * NA_KC - 2)
    onehot = (np.arange(2 * NA_KC - 1)[:, None, None] == dc[None]).astype(np.float32)
    toep = jnp.einsum('hdm,mqc->hdqc', rpb.astype(F32), onehot, precision=lax.Precision.HIGHEST)
    toep = jnp.where(ok[None, None], toep, NEG_INF)
    t = jnp.stack([toep[:, NA_KR - 1 - v:2 * NA_KR - 1 - v] for v in range(NA_KR)], axis=1)
    t = jnp.transpose(t, (0, 1, 3, 2, 4))
    return t.reshape(rpb.shape[0], NA_KR, w, NA_KR * w)


def _na_kernel(q_ref, kp_ref, kc_ref, kn_ref, vp_ref, vc_ref, vn_ref, kx_ref, vx_ref, bias_ref,
               o_ref, ks_ref, vs_ref, *, rows):
    w = GRID_W
    blk = NA_RB * w
    i = pl.program_id(2)
    ks_ref[0:blk, :] = kp_ref[...].astype(BF16)
    ks_ref[blk:2 * blk, :] = kc_ref[...].astype(BF16)
    ks_ref[2 * blk:3 * blk, :] = kn_ref[...].astype(BF16)
    vs_ref[0:blk, :] = vp_ref[...].astype(BF16)
    vs_ref[blk:2 * blk, :] = vc_ref[...].astype(BF16)
    vs_ref[2 * blk:3 * blk, :] = vn_ref[...].astype(BF16)
    scale = NA_DH ** -0.5
    first_head = lax.broadcasted_iota(I32, (w, 2 * NA_DH), 1) < NA_DH
    qbd = []
    for rr in range(NA_RB):
        q2 = q_ref[rr * w:(rr + 1) * w, :] * scale
        qbd.append(jnp.concatenate([jnp.where(first_head, q2, 0.0), jnp.where(first_head, 0.0, q2)],
                                   axis=0).astype(BF16))
    sx = _dot_nt(jnp.concatenate(qbd, axis=0), kx_ref[...].astype(BF16))
    mx = jnp.max(sx, axis=-1, keepdims=True)
    px = jnp.exp(sx - mx)
    dx = jnp.sum(px, axis=-1, keepdims=True)
    ox = _dot(px.astype(BF16), vx_ref[...].astype(BF16))
    for rr in range(NA_RB):
        r = i * NA_RB + rr
        r0 = jnp.clip(r - NA_KR // 2, 0, rows - NA_KR)
        start = pl.multiple_of((r0 - (i - 1) * NA_RB) * w, w)
        vv = r - r0
        s = _dot_nt(qbd[rr], ks_ref[pl.ds(start, NA_KR * w), :])
        s = s + jnp.concatenate([bias_ref[0, vv], bias_ref[1, vv]], axis=0)
        mw = jnp.max(s, axis=-1, keepdims=True)
        p = jnp.exp(s - mw)
        dw = jnp.sum(p, axis=-1, keepdims=True)
        ow = _dot(p.astype(BF16), vs_ref[pl.ds(start, NA_KR * w), :])
        rs = slice(rr * 2 * w, (rr + 1) * 2 * w)
        m = jnp.maximum(mw, mx[rs])
        aw = jnp.exp(mw - m)
        ax = jnp.exp(mx[rs] - m)
        o = (ow * aw + ox[rs] * ax) / (dw * aw + dx[rs] * ax)
        o_ref[rr * w:(rr + 1) * w, :] = jnp.where(first_head, o[0:w], o[w:2 * w]).astype(o_ref.dtype)


def neighborhood_attention(p_lat, p_ctx, bias_tbl, *, bsz, seq, lc, col0):
    w = GRID_W
    rows = seq // w
    blk = NA_RB * w
    nb = rows // NA_RB
    hp = NA_HEADS // 2
    pw = 2 * NA_DH
    cq = col0 // pw
    ck = cq + hp
    cv = ck + hp

    def lat(col, off):
        def im(b, h, i):
            return (b * nb + jnp.clip(i + off, 0, nb - 1), col + h)
        return pl.BlockSpec((blk, pw), im)

    def ctx(col):
        return pl.BlockSpec((lc, pw), lambda b, h, i: (b, col + h))

    return pl.pallas_call(
        functools.partial(_na_kernel, rows=rows),
        grid=(bsz, hp, nb),
        in_specs=[lat(cq, 0), lat(ck, -1), lat(ck, 0), lat(ck, 1), lat(cv, -1), lat(cv, 0), lat(cv, 1),
                  ctx(ck), ctx(cv),
                  pl.BlockSpec((2, NA_KR, w, NA_KR * w), lambda b, h, i: (h, 0, 0, 0))],
        out_specs=pl.BlockSpec((blk, pw), lambda b, h, i: (b * nb + i, h)),
        out_shape=jax.ShapeDtypeStruct((bsz * seq, NA_WIDTH), BF16),
        scratch_shapes=[pltpu.VMEM((3 * blk, pw), BF16), pltpu.VMEM((3 * blk, pw), BF16)],
        compiler_params=_cparams("parallel", "parallel", "arbitrary"),
        name="neighborhood_attention",
    )(p_lat, p_lat, p_lat, p_lat, p_lat, p_lat, p_lat, p_ctx, p_ctx, bias_tbl)


def _ctx_attn_kernel(q_ref, k_ref, v_ref, o_ref):
    scale = NA_DH ** -0.5
    outs = []
    for hh in range(2):
        cs = slice(hh * NA_DH, (hh + 1) * NA_DH)
        q = (q_ref[:, cs] * scale).astype(BF16)
        s = _dot_nt(q, k_ref[:, cs].astype(BF16))
        p = jnp.exp(s - jnp.max(s, axis=-1, keepdims=True))
        o = _dot(p.astype(BF16), v_ref[:, cs].astype(BF16))
        outs.append(o / jnp.sum(p, axis=-1, keepdims=True))
    o_ref[...] = jnp.concatenate(outs, axis=-1).astype(o_ref.dtype)


def context_attention(p_ctx, *, bsz, lc, col0):
    hp = NA_HEADS // 2
    pw = 2 * NA_DH
    cq = col0 // pw

    def spec(col):
        return pl.BlockSpec((lc, pw), lambda b, h: (b, col + h))

    return pl.pallas_call(
        _ctx_attn_kernel,
        grid=(bsz, hp),
        in_specs=[spec(cq), spec(cq + hp), spec(cq + 2 * hp)],
        out_specs=pl.BlockSpec((lc, pw), lambda b, h: (b, h)),
        out_shape=jax.ShapeDtypeStruct((bsz * lc, NA_WIDTH), BF16),
        compiler_params=_cparams("parallel", "parallel"),
        name="context_attention",
    )(p_ctx, p_ctx, p_ctx)


def _residual_router(x, y, gate, g1, sh, sc, wrt, xa_ref, aff_ref):
    x1 = x + gate * y
    n, d = x1.shape
    _store_packed(xa_ref, x1, n, 0, d)
    _store_packed(xa_ref, x1, n, 1, d)
    pitch = _row_pitch(d)
    for k in range(2 * (d // LANES), pitch):
        xa_ref[pl.ds(k, n, stride=pitch), :] = jnp.zeros((n, LANES), F32)
    f = _rms_mod(x1, g1, sh, sc)
    lt = _dot3_nt(wrt, f)
    e = jnp.exp(lt - jnp.max(lt, axis=0, keepdims=True))
    aff_ref[0] = e / jnp.sum(e, axis=0, keepdims=True)


def _ab_out_kernel(of_ref, ob_ref, gg_ref, nl_ref, on_ref, w_ref, x_ref, gate_ref, g1_ref, sh_ref, sc_ref, wrt_ref,
                   xa_ref, aff_ref, *, packed):
    o = of_ref[...] + ob_ref[...]
    gain = on_ref[...]
    parts = []
    for h in range(HGRN_HEADS):
        oh = o[:, h * HGRN_DK:(h + 1) * HGRN_DK]
        parts.append(oh * lax.rsqrt(jnp.mean(oh * oh, axis=-1, keepdims=True) + RMS_EPS) * gain)
    hl = (jnp.concatenate(parts, axis=-1) * _silu(gg_ref[...])).astype(BF16)
    y = _dot(hl, w_ref[0:HGRN_WIDTH, :]) + _dot(nl_ref[...], w_ref[HGRN_WIDTH:, :])
    _residual_router(_load_stream(x_ref, packed), y, gate_ref[0], g1_ref[...], sh_ref[0], sc_ref[0], wrt_ref[...],
                     xa_ref, aff_ref)


def _ssd_out_kernel(yf_ref, yb_ref, xs_ref, z_ref, dsk_ref, ng_ref, w_ref, x_ref, gate_ref, g1_ref, sh_ref, sc_ref,
                    wrt_ref, xa_ref, aff_ref, *, group_width, packed):
    y = (yf_ref[...] + yb_ref[...] + dsk_ref[...] * xs_ref[...]) * _silu(z_ref[...])
    parts = []
    for g in range(y.shape[1] // group_width):
        yg = y[:, g * group_width:(g + 1) * group_width]
        parts.append(yg * lax.rsqrt(jnp.mean(yg * yg, axis=-1, keepdims=True) + RMS_EPS))
    yn = (jnp.concatenate(parts, axis=-1) * ng_ref[...]).astype(BF16)
    _residual_router(_load_stream(x_ref, packed), _dot(yn, w_ref[...]), gate_ref[0], g1_ref[...], sh_ref[0],
                     sc_ref[0], wrt_ref[...], xa_ref, aff_ref)


def _out_common(x, gate, g1, sh, sc, wrt, *, rows, rows_per_mod, tm, packed):
    r = rows
    e, d = wrt.shape
    per = rows_per_mod // tm
    nbatch = r // rows_per_mod
    pitch = _row_pitch(d)
    mod = lambda i: (i // per, 0, 0)
    in_specs = [_stream_spec(tm, d, packed, lambda i: i), pl.BlockSpec((1, 1, d), mod),
                pl.BlockSpec((1, d), lambda i: (0, 0)),
                pl.BlockSpec((1, 1, d), mod), pl.BlockSpec((1, 1, d), mod), pl.BlockSpec((e, d), lambda i: (0, 0))]
    out_specs = [pl.BlockSpec((tm * pitch, LANES), lambda i: (i, 0)),
                 pl.BlockSpec((1, e, tm), lambda i: (i // per, 0, i % per))]
    out_shape = [jax.ShapeDtypeStruct((r * pitch, LANES), F32), jax.ShapeDtypeStruct((nbatch, e, rows_per_mod), F32)]
    args = (x, gate, g1.reshape(1, d), sh, sc, wrt)
    return in_specs, out_specs, out_shape, args


def ab_out(o_f, o_b, p, nl, onorm_g, w_out, x, gate, g1, sh, sc, wrt, *, rows_per_mod, tm, packed):
    tm = min(tm, rows_per_mod)
    r = o_f.shape[0]
    wd = HGRN_WIDTH
    row = lambda i: (i, 0)
    d = wrt.shape[1]
    c_in, c_out, c_shape, c_args = _out_common(x, gate, g1, sh, sc, wrt, rows=r, rows_per_mod=rows_per_mod, tm=tm,
                                               packed=packed)
    in_specs = [pl.BlockSpec((tm, wd), row), pl.BlockSpec((tm, wd), row),
                pl.BlockSpec((tm, wd), lambda i: (i, 4)),
                pl.BlockSpec((tm, NA_WIDTH), row),
                pl.BlockSpec((1, HGRN_DK), lambda i: (0, 0)),
                pl.BlockSpec(w_out.shape, lambda i: (0, 0))] + c_in
    return pl.pallas_call(
        functools.partial(_ab_out_kernel, packed=(tm, d) if packed else None),
        grid=(r // tm,), in_specs=in_specs, out_specs=c_out, out_shape=c_shape,
        compiler_params=_cparams("parallel"), name="ab_out",
    )(o_f, o_b, p, nl, onorm_g.reshape(1, HGRN_DK), w_out, *c_args)


def ssd_out(y_f, y_b, xa, p, dsk, ng, w_out, x, gate, g1, sh, sc, wrt, *, rows_per_mod, tm, packed):
    tm = min(tm, rows_per_mod)
    r = y_f.shape[0]
    inner = w_out.shape[0]
    row = lambda i: (i, 0)
    d = wrt.shape[1]
    c_in, c_out, c_shape, c_args = _out_common(x, gate, g1, sh, sc, wrt, rows=r, rows_per_mod=rows_per_mod, tm=tm,
                                               packed=packed)
    in_specs = [pl.BlockSpec((tm, inner), row), pl.BlockSpec((tm, inner), row),
                pl.BlockSpec((tm, inner), row),
                pl.BlockSpec((tm, inner), row),
                pl.BlockSpec((1, inner), lambda i: (0, 0)), pl.BlockSpec((1, inner), lambda i: (0, 0)),
                pl.BlockSpec(w_out.shape, lambda i: (0, 0))] + c_in
    return pl.pallas_call(
        functools.partial(_ssd_out_kernel, group_width=inner // SSD_GROUPS, packed=(tm, d) if packed else None),
        grid=(r // tm,), in_specs=in_specs, out_specs=c_out, out_shape=c_shape,
        compiler_params=_cparams("parallel"), name="ssd_out",
    )(y_f, y_b, xa, p, dsk.reshape(1, inner), ng.reshape(1, inner), w_out, *c_args)


F32_INF_BITS = 0x7F800000


def _lane_cumsum(src_ref, dst_ref, n_chunks):
    e = src_ref.shape[0]
    upper = _tri(LANES, True).astype(BF16)

    def body(j, off):
        ds = pl.ds(pl.multiple_of(j * LANES, LANES), LANES)
        c = _dot(src_ref[:, ds].astype(BF16), upper) + off
        dst_ref[:, ds] = c
        return c[:, LANES - 1:LANES]

    lax.fori_loop(0, n_chunks, body, jnp.zeros((e, 1), F32))


def _select_kernel(aff_ref, idx_ref, gate_ref, m_scr, c_scr, *, seq, cap, nph, kc, row_scale):
    e = aff_ref.shape[1]
    n_chunks = seq // LANES
    bits = pltpu.bitcast(aff_ref[0], I32)

    def bisect(_, lohi):
        lo, hi = lohi
        mid = lo + lax.shift_right_logical(hi - lo, 1)
        cnt = jnp.sum((bits >= mid).astype(F32), axis=1, keepdims=True)
        ge = cnt >= cap
        return jnp.where(ge, mid, lo), jnp.where(ge, hi, mid)

    thr, _ = lax.fori_loop(0, 31, bisect, (jnp.zeros((e, 1), I32), jnp.full((e, 1), F32_INF_BITS, I32)))
    gt = bits > thr
    eq = bits == thr
    need = cap - jnp.sum(gt.astype(F32), axis=1, keepdims=True)
    m_scr[...] = eq.astype(F32)
    _lane_cumsum(m_scr, c_scr, n_chunks)
    sel = gt | (eq & (c_scr[...] <= need))
    m_scr[...] = sel.astype(F32)
    _lane_cumsum(m_scr, c_scr, n_chunks)

    sub_p = lax.broadcasted_iota(I32, (nph, kc), 0).astype(F32)
    sub_l = lax.broadcasted_iota(I32, (LANES, kc), 0).astype(F32)
    lane_t = lax.broadcasted_iota(I32, (1, kc), 1)
    inv = 1.0 / LANES
    for ex in range(e):
        def body(c, acc):
            ds = pl.ds(pl.multiple_of(c * kc, LANES), kc)
            pos = c_scr[ex:ex + 1, ds] - 1.0
            phi = jnp.floor(pos * inv)
            plo = pos - LANES * phi
            hit = (phi == sub_p) & (m_scr[ex:ex + 1, ds] > 0.0)
            t = (lane_t + c * kc).astype(F32)
            th = jnp.floor(t * inv)
            tl = t - LANES * th
            a1, a2, a3 = _split3(aff_ref[0, ex:ex + 1, ds])
            rows = [th, tl, a1.astype(F32), a2.astype(F32), a3.astype(F32)]
            lhs = jnp.concatenate([jnp.where(hit, r, 0.0) for r in rows], axis=0).astype(BF16)
            onehot = (plo == sub_l).astype(BF16)
            return acc + _dot_nt(lhs, onehot)

        acc = lax.fori_loop(0, seq // kc, body, jnp.zeros((5 * nph, LANES), F32))
        token = (acc[0:nph] * LANES + acc[nph:2 * nph]).astype(I32)
        idx_ref[0, ex] = (pl.program_id(0) * seq + token) * row_scale
        gate_ref[0, ex] = acc[2 * nph:3 * nph] + acc[3 * nph:4 * nph] + acc[4 * nph:5 * nph]


def expert_select(aff_t, cap, row_scale):
    bsz, e, seq = aff_t.shape
    nph = -(-cap // LANES)
    nph = -(-nph // SUBLANES) * SUBLANES
    kc = min(2048, seq)
    idx, gate = pl.pallas_call(
        functools.partial(_select_kernel, seq=seq, cap=cap, nph=nph, kc=kc, row_scale=row_scale),
        grid=(bsz,),
        in_specs=[pl.BlockSpec((1, e, seq), lambda b: (b, 0, 0))],
        out_specs=[pl.BlockSpec((1, e, nph, LANES), lambda b: (b, 0, 0, 0)),
                   pl.BlockSpec((1, e, nph, LANES), lambda b: (b, 0, 0, 0))],
        out_shape=[jax.ShapeDtypeStruct((bsz, e, nph, LANES), I32),
                   jax.ShapeDtypeStruct((bsz, e, nph, LANES), F32)],
        scratch_shapes=[pltpu.VMEM((e, seq), F32), pltpu.VMEM((e, seq), F32)],
        compiler_params=_cparams("parallel"),
        name="expert_select",
    )(aff_t)
    return idx.reshape(bsz, e, nph * LANES)[:, :, :cap], gate.reshape(bsz, e, nph * LANES)[:, :, :cap]


FFN_SLOTS = 3
FFN_TILE = 512


def _ffn_kernel(idx_ref, gate_ref, g1_ref, sh_ref, sc_ref, g5_ref, w1_ref, w3_ref, w2_ref, xa_in,
                xa_hbm, buf, gsem, ssem, *, tr, nt, d):
    del xa_in
    nx = d // LANES
    pitch = _row_pitch(d)
    grp = min(8, tr)
    j = pl.program_id(2)
    step = (pl.program_id(0) * pl.num_programs(1) + pl.program_id(1)) * nt + j
    slot = step % FFN_SLOTS
    nslot = (step + 1) % FFN_SLOTS

    def start_gathers(tile, sl):
        def body(i, c):
            for k in range(grp):
                src = idx_ref[0, 0, tile * tr + i * grp + k]
                dst = i * (grp * pitch) + k * pitch
                pltpu.make_async_copy(xa_hbm.at[pl.ds(src, 2 * nx)], buf.at[sl, pl.ds(dst, 2 * nx)],
                                      gsem.at[sl]).start()
            return c
        lax.fori_loop(0, tr // grp, body, 0)

    def start_scatters(tile, sl):
        def body(i, c):
            for k in range(grp):
                dst = idx_ref[0, 0, tile * tr + i * grp + k] + nx
                src = i * (grp * pitch) + k * pitch + nx
                pltpu.make_async_copy(buf.at[sl, pl.ds(src, nx)], xa_hbm.at[pl.ds(dst, nx)], ssem.at[sl]).start()
            return c
        lax.fori_loop(0, tr // grp, body, 0)

    def wait_gathers(sl):
        n = tr * 2 * nx
        pltpu.make_async_copy(xa_hbm.at[pl.ds(0, n)], buf.at[sl, pl.ds(0, n)], gsem.at[sl]).wait()

    def wait_scatters(sl):
        n = tr * nx
        pltpu.make_async_copy(buf.at[sl, pl.ds(0, n)], xa_hbm.at[pl.ds(0, n)], ssem.at[sl]).wait()

    @pl.when(j == 0)
    def _():
        start_gathers(0, slot)

    @pl.when(j + 1 < nt)
    def _():
        @pl.when(j >= FFN_SLOTS - 1)
        def _():
            wait_scatters(nslot)

        start_gathers(j + 1, nslot)

    wait_gathers(slot)
    rows = buf.at[slot]
    f = _rms_mod(_load_packed(rows, tr, 0, d), g1_ref[...], sh_ref[0], sc_ref[0]).astype(BF16)
    hid = (_silu(_dot(f, w1_ref[0, 0])) * _dot(f, w3_ref[0, 0])).astype(BF16)
    y = _dot(hid, w2_ref[0, 0])
    eye = lax.broadcasted_iota(I32, (tr, tr), 0) == lax.broadcasted_iota(I32, (tr, tr), 1)
    gcol = jnp.sum(jnp.where(eye, gate_ref[0], 0.0), axis=1, keepdims=True)
    _store_packed(rows, _load_packed(rows, tr, 1, d) + (g5_ref[0] * gcol) * y, tr, 1, d)
    start_scatters(j, slot)

    @pl.when(j == nt - 1)
    def _():
        for back in range(min(FFN_SLOTS, nt)):
            wait_scatters((step - back) % FFN_SLOTS)


def expert_ffn_combine(xa, idx, gate, g1, sh, sc, g5, w1, w3, w2, *, layer):
    bsz, e, cap = idx.shape
    d, ff = w1.shape[2], w1.shape[3]
    tr = min(FFN_TILE, cap)
    nt = cap // tr
    mod = lambda b, ex, j: (b, 0, 0)
    return pl.pallas_call(
        functools.partial(_ffn_kernel, tr=tr, nt=nt, d=d),
        grid=(bsz, e, nt),
        in_specs=[pl.BlockSpec((1, 1, cap), lambda b, ex, j: (b * e + ex, 0, 0), memory_space=pltpu.SMEM),
                  pl.BlockSpec((1, 1, tr), lambda b, ex, j: ((b * e + ex) * nt + j, 0, 0)),
                  pl.BlockSpec((1, d), lambda b, ex, j: (0, 0)),
                  pl.BlockSpec((1, 1, d), mod), pl.BlockSpec((1, 1, d), mod), pl.BlockSpec((1, 1, d), mod),
                  pl.BlockSpec((1, 1, d, ff), lambda b, ex, j: (layer, ex, 0, 0)),
                  pl.BlockSpec((1, 1, d, ff), lambda b, ex, j: (layer, ex, 0, 0)),
                  pl.BlockSpec((1, 1, ff, d), lambda b, ex, j: (layer, ex, 0, 0)),
                  pl.BlockSpec(memory_space=pl.ANY)],
        out_specs=pl.BlockSpec(memory_space=pl.ANY),
        out_shape=jax.ShapeDtypeStruct(xa.shape, F32),
        scratch_shapes=[pltpu.VMEM((FFN_SLOTS, tr * _row_pitch(d), LANES), F32),
                        pltpu.SemaphoreType.DMA((FFN_SLOTS,)), pltpu.SemaphoreType.DMA((FFN_SLOTS,))],
        input_output_aliases={9: 0},
        compiler_params=_cparams("arbitrary", "arbitrary", "arbitrary"),
        name="expert_ffn_combine",
    )(idx.reshape(bsz * e, 1, cap), gate.reshape(bsz * e * nt, 1, tr), g1.reshape(1, d), sh, sc, g5,
      w1, w3, w2, xa)


CONV_HALO = 8


def _ssd_in_kernel(cur_ref, prev_ref, next_ref, g_ref, sh_ref, sc_ref, w_ref, cw_ref, cb_ref, dtb_ref,
                   z_ref, xa_ref, dtv_ref, ext_ref, *, tm, per, inner, cdim, packed):
    i = pl.program_id(0)
    first = (i % per) == 0
    last = (i % per) == per - 1
    halo = (CONV_HALO, packed[1]) if packed else None

    def normed(ref, pk):
        return _rms_mod(_load_stream(ref, pk), g_ref[...], sh_ref[0], sc_ref[0]).astype(BF16)

    h = normed(cur_ref, packed)
    hp = normed(prev_ref, halo)
    hn = normed(next_ref, halo)
    z_ref[...] = _dot(h, w_ref[:, 0:inner])
    v = _dot(h, w_ref[:, inner + cdim:inner + cdim + LANES]) + dtb_ref[...]
    dtv_ref[...] = jnp.maximum(v, 0.0) + jnp.log1p(jnp.exp(-jnp.abs(v)))
    half = cdim // 2
    lo = CONV_HALO - SSD_CONV // 2
    for j in range(2):
        cols = slice(inner + j * half, inner + (j + 1) * half)
        out = slice(j * half, (j + 1) * half)
        ext_ref[0:CONV_HALO, :] = jnp.where(first, 0.0, _dot(hp, w_ref[:, cols]))
        ext_ref[CONV_HALO:CONV_HALO + tm, :] = _dot(h, w_ref[:, cols])
        ext_ref[CONV_HALO + tm:2 * CONV_HALO + tm, :] = jnp.where(last, 0.0, _dot(hn, w_ref[:, cols]))
        acc = cb_ref[:, out] + cw_ref[0:1, out] * ext_ref[pl.ds(lo, tm), :]
        for k in range(1, SSD_CONV):
            acc = acc + cw_ref[k:k + 1, out] * ext_ref[pl.ds(lo + k, tm), :]
        xa_ref[:, out] = _silu(acc)


def ssd_in(x, g, shift, scale, w, conv_w, conv_b, dt_bias, *, rows, rows_per_seq, tm, inner, packed):
    r = rows
    d = w.shape[0]
    cdim = conv_w.shape[1]
    tm = min(tm, rows_per_seq)
    per = rows_per_seq // tm
    hb = tm // CONV_HALO
    nhb = r // CONV_HALO
    const = lambda i: (0, 0)
    mod = lambda i: (i // per, 0, 0)
    return pl.pallas_call(
        functools.partial(_ssd_in_kernel, tm=tm, per=per, inner=inner, cdim=cdim,
                          packed=(tm, d) if packed else None),
        grid=(r // tm,),
        in_specs=[_stream_spec(tm, d, packed, lambda i: i),
                  _stream_spec(CONV_HALO, d, packed, lambda i: jnp.maximum(i * hb - 1, 0)),
                  _stream_spec(CONV_HALO, d, packed, lambda i: jnp.minimum((i + 1) * hb, nhb - 1)),
                  pl.BlockSpec((1, d), const), pl.BlockSpec((1, 1, d), mod), pl.BlockSpec((1, 1, d), mod),
                  pl.BlockSpec(w.shape, const), pl.BlockSpec((SSD_CONV, cdim), const),
                  pl.BlockSpec((1, cdim), const), pl.BlockSpec((1, LANES), const)],
        out_specs=[pl.BlockSpec((tm, inner), lambda i: (i, 0)), pl.BlockSpec((tm, cdim), lambda i: (i, 0)),
                   pl.BlockSpec((tm, LANES), lambda i: (i, 0))],
        out_shape=[jax.ShapeDtypeStruct((r, inner), F32), jax.ShapeDtypeStruct((r, cdim), F32),
                   jax.ShapeDtypeStruct((r, LANES), F32)],
        scratch_shapes=[pltpu.VMEM((tm + 2 * CONV_HALO, cdim // 2), F32)],
        compiler_params=_cparams("parallel"),
        name="ssd_in",
    )(x, x, x, g.reshape(1, d), shift, scale, w, conv_w, conv_b.reshape(1, cdim), dt_bias)


SSD_HPG = 4


def _ssd_block(x, bm, cm, dtv, a_row, hs, lane0, upper, need_y=True):
    t = x.shape[0]
    hp = x.shape[1]
    pdim = hp // SSD_HPG
    ch = SSD_CHUNK
    nc = t // ch
    lane = lax.broadcasted_iota(I32, (1, LANES), 1)
    head_of_lane = lax.broadcasted_iota(I32, (1, hp), 1) // pdim
    dtx = jnp.zeros((t, hp), F32)
    a_x = jnp.zeros((1, hp), F32)
    for h in range(SSD_HPG):
        pick = lane == lane0 + h
        dtx = jnp.where(head_of_lane == h, jnp.sum(jnp.where(pick, dtv, 0.0), axis=1, keepdims=True), dtx)
        a_x = jnp.where(head_of_lane == h, jnp.sum(jnp.where(pick, a_row, 0.0), axis=1, keepdims=True), a_x)
    dtax = dtx * a_x
    c3 = _dot(_block_tri(t, ch, upper).astype(BF16), jnp.concatenate(_split3(dtax), axis=1))
    acx = c3[:, 0:hp] + c3[:, hp:2 * hp] + c3[:, 2 * hp:3 * hp]
    r = lax.broadcasted_iota(I32, (ch, hp), 0)
    s = lax.broadcasted_iota(I32, (ch, hp), 1) % ch
    src_le = (s <= r) if upper else (s >= r)
    ident = s == r
    causal = (r <= s) if upper else (r >= s)
    same_head = (lax.broadcasted_iota(I32, (hp, hp), 0) // pdim) == (lax.broadcasted_iota(I32, (hp, hp), 1) // pdim)
    ydiag, decay_out, alasts, upds = [], [], [], []
    for c in range(nc):
        rs = slice(c * ch, (c + 1) * ch)
        acx_c, dtx_c, x_c = acx[rs], dtx[rs], x[rs]
        bm_c = bm[rs].astype(BF16)
        alast = acx_c[0:1] if upper else acx_c[ch - 1:ch]
        if need_y:
            arx = jnp.sum(jnp.where(src_le, dtax[rs], 0.0), axis=0, keepdims=True)
            dtr = jnp.sum(jnp.where(ident, dtx_c, 0.0), axis=0, keepdims=True)
            dec = jnp.where(causal, jnp.exp(jnp.minimum(acx_c - arx, 0.0)), 0.0)
            cbx = _dot_nt(cm[rs].astype(BF16), jnp.concatenate([bm_c] * SSD_HPG, axis=0))
            gx = (cbx * dec * dtr).astype(BF16)
            xbd = jnp.where(same_head, jnp.concatenate([x_c] * SSD_HPG, axis=0), 0.0).astype(BF16)
            ydiag.append(_dot(gx, xbd))
            decay_out.append(jnp.exp(acx_c))
        alasts.append(alast)
        upds.append(_dot_tn(bm_c, (x_c * (jnp.exp(alast - acx_c) * dtx_c)).astype(BF16)))
    ys = [None] * nc
    for c in (reversed(range(nc)) if upper else range(nc)):
        if need_y:
            ys[c] = ydiag[c] + _dot(cm[c * ch:(c + 1) * ch].astype(BF16), hs.astype(BF16)) * decay_out[c]
        hs = hs * jnp.exp(alasts[c]) + upds[c]
    return (jnp.concatenate(ys, axis=0) if need_y else None), hs


def _ssd_kernel(xf_ref, bf_ref, cf_ref, df_ref, xb_ref, bb_ref, cb_ref, db_ref,
                xc_ref, bc_ref, cc_ref, dc_ref, alog_ref, yf_ref, yb_ref, hf_ref, hb_ref):
    g = pl.program_id(0) % SSD_GROUPS
    a_row = -jnp.exp(alog_ref[...])
    lane_f = g * SSD_HPG
    lane_b = SSD_GROUPS * SSD_HPG + g * SSD_HPG

    def run(x_ref, b_ref, c_ref, d_ref, y_ref, hs, lane0, upper):
        n = x_ref.shape[0]
        sub = min(SCAN_SUB, n)
        order = range(n // sub)
        for sb in (reversed(order) if upper else order):
            rs = slice(sb * sub, (sb + 1) * sub)
            y, hs = _ssd_block(x_ref[rs, :], b_ref[rs, :], c_ref[rs, :], d_ref[rs, :], a_row, hs, lane0, upper,
                               need_y=y_ref is not None)
            if y_ref is not None:
                y_ref[rs, :] = y
        return hs

    @pl.when(pl.program_id(1) == 0)
    def _():
        z0 = jnp.zeros(hf_ref.shape, F32)
        hf_ref[...] = run(xc_ref, bc_ref, cc_ref, dc_ref, None, z0, lane_f, False)
        hb_ref[...] = run(xc_ref, bc_ref, cc_ref, dc_ref, None, z0, lane_b, True)

    hf_ref[...] = run(xf_ref, bf_ref, cf_ref, df_ref, yf_ref, hf_ref[...], lane_f, False)
    hb_ref[...] = run(xb_ref, bb_ref, cb_ref, db_ref, yb_ref, hb_ref[...], lane_b, True)


def ssd_scan(xa, dtv, xa_c, dtv_c, a_log_row, *, bsz, seq, lc, tb, inner):
    g = SSD_GROUPS
    n = SSD_STATE
    xw = inner // g
    nb = seq // tb
    cb0 = inner // n
    cc0 = cb0 + g

    def blk(i, rev):
        return nb - 1 - i if rev else i

    def lat(width, col, rev):
        return pl.BlockSpec((tb, width), lambda bg, i: ((bg // g) * nb + blk(i, rev), col + bg % g))

    def lat_dt(rev):
        return pl.BlockSpec((tb, LANES), lambda bg, i: ((bg // g) * nb + blk(i, rev), 0))

    def ctx(width, col):
        return pl.BlockSpec((lc, width), lambda bg, i: (bg // g, col + bg % g))

    def out(rev):
        return pl.BlockSpec((tb, xw), lambda bg, i: ((bg // g) * nb + blk(i, rev), bg % g))

    return pl.pallas_call(
        _ssd_kernel,
        grid=(bsz * g, nb),
        in_specs=[lat(xw, 0, False), lat(n, cb0, False), lat(n, cc0, False), lat_dt(False),
                  lat(xw, 0, True), lat(n, cb0, True), lat(n, cc0, True), lat_dt(True),
                  ctx(xw, 0), ctx(n, cb0), ctx(n, cc0), pl.BlockSpec((lc, LANES), lambda bg, i: (bg // g, 0)),
                  pl.BlockSpec((1, LANES), lambda bg, i: (0, 0))],
        out_specs=[out(False), out(True)],
        out_shape=[jax.ShapeDtypeStruct((bsz * seq, inner), F32), jax.ShapeDtypeStruct((bsz * seq, inner), F32)],
        scratch_shapes=[pltpu.VMEM((n, xw), F32), pltpu.VMEM((n, xw), F32)],
        compiler_params=_cparams("parallel", "arbitrary", flags=SCAN_FLAGS),
        name="ssd_scan",
    )(xa, xa, xa, dtv, xa, xa, xa, dtv, xa_c, xa_c, xa_c, dtv_c, a_log_row)


def _final_norm_kernel(x_ref, g_ref, o_ref, *, packed):
    x = _load_stream(x_ref, packed)
    o_ref[...] = x * lax.rsqrt(jnp.mean(x * x, axis=-1, keepdims=True) + RMS_EPS) * g_ref[...]


def final_norm(x, g, *, rows, tm, packed):
    r = rows
    d = g.shape[0]
    return pl.pallas_call(
        functools.partial(_final_norm_kernel, packed=(tm, d) if packed else None),
        grid=(r // tm,),
        in_specs=[_stream_spec(tm, d, packed, lambda i: i), pl.BlockSpec((1, d), lambda i: (0, 0))],
        out_specs=pl.BlockSpec((tm, d), lambda i: (i, 0)),
        out_shape=jax.ShapeDtypeStruct((r, d), F32),
        compiler_params=_cparams("parallel"),
        name="final_norm",
    )(x, g.reshape(1, d))


def _moe(xa, aff, g1, sh, sc, g5, w1, w3, w2, *, seq, layer):
    cap = max(1, CAPACITY_FACTOR * seq // N_EXPERTS)
    idx, gate = expert_select(aff, cap, _row_pitch(w1.shape[2]))
    return expert_ffn_combine(xa, idx, gate, g1, sh, sc, g5, w1, w3, w2, layer=layer)


def kernel(x, c, ctx, c_ctx, ada_w, ada_b, norm_g, final_g, ab_w_in, ab_w_out, hgrn_lb_logits, hgrn_onorm_g, na_rpb, ssd_w_in, ssd_conv_w, ssd_conv_b, ssd_a_log, ssd_dt_bias, ssd_d, ssd_norm_g, ssd_w_out, moe_router, moe_w1, moe_w3, moe_w2):
    bsz, seq, d = x.shape
    lc = ctx.shape[1]
    depth = ada_w.shape[0]
    assert depth == 2, "context outputs of an SSD layer are not implemented (only needed when a layer follows it)"
    mods = modulation_table(c, c_ctx, ada_w, ada_b)
    lb_all = jnp.cumsum(jax.nn.softmax(hgrn_lb_logits.astype(F32), axis=1), axis=1)
    xl = x.reshape(bsz * seq, d)
    xc = ctx.reshape(bsz * lc, d)
    packed = False
    w1, w3, w2 = (w.astype(BF16) for w in (moe_w1, moe_w3, moe_w2))
    hg = 5 * HGRN_WIDTH
    for l in range(depth):
        need_ctx = l < depth - 1
        k = l // 2
        m = mods[l]
        ml = [m[:bsz, j * d:(j + 1) * d].reshape(bsz, 1, d) for j in range(6)]
        mc = [jnp.broadcast_to(m[bsz, j * d:(j + 1) * d].reshape(1, 1, d), (bsz, 1, d)) for j in range(6)]
        wrt = moe_router[l].T
        g0, g1 = norm_g[l, 0], norm_g[l, 1]
        if l % 2 == 0:
            w_in = ab_w_in[k].astype(BF16)
            w_out = ab_w_out[k].astype(BF16)
            p_lat = norm_mod_project(xl, g0, ml[0], ml[1], w_in, rows=bsz * seq, rows_per_mod=seq, tm=512, tn=1024,
                                     packed=packed)
            p_ctx = norm_mod_project(xc, g0, mc[0], mc[1], w_in, rows=bsz * lc, rows_per_mod=lc, tm=512, tn=1024,
                                     packed=packed)
            o_f, o_b, oc_f, oc_b = hgrn_scan(p_lat, p_ctx, lb_all[:, k], bsz=bsz, seq=seq, lc=lc, tb=min(512, seq))
            nl = neighborhood_attention(p_lat, p_ctx, na_bias_table(na_rpb[k]), bsz=bsz, seq=seq, lc=lc, col0=hg)
            xa, aff = ab_out(o_f, o_b, p_lat, nl, hgrn_onorm_g[k], w_out, xl, ml[2], g1, ml[3], ml[4], wrt,
                             rows_per_mod=seq, tm=512, packed=packed)
            if need_ctx:
                nc = context_attention(p_ctx, bsz=bsz, lc=lc, col0=hg)
                ca, caff = ab_out(oc_f, oc_b, p_ctx, nc, hgrn_onorm_g[k], w_out, xc, mc[2], g1, mc[3], mc[4],
                                  wrt, rows_per_mod=lc, tm=512, packed=packed)
        else:
            inner = ssd_w_out.shape[1]
            heads2 = ssd_dt_bias.shape[1] * ssd_dt_bias.shape[2]
            wz = ssd_w_in[k]
            pad = jnp.zeros((d, LANES - heads2), F32)
            w_in = jnp.concatenate([wz, pad], axis=1).astype(BF16)
            w_out = ssd_w_out[k].astype(BF16)
            dtb = jnp.zeros((1, LANES), F32).at[0, :heads2].set(ssd_dt_bias[k].reshape(-1))
            alog = jnp.zeros((1, LANES), F32).at[0, :heads2].set(ssd_a_log[k].reshape(-1))
            proj = functools.partial(ssd_in, g=g0, w=w_in, conv_w=ssd_conv_w[k], conv_b=ssd_conv_b[k], dt_bias=dtb,
                                     tm=256, inner=inner, packed=packed)
            z, xs, dtv = proj(xl, shift=ml[0], scale=ml[1], rows=bsz * seq, rows_per_seq=seq)
            _, xs_c, dtv_c = proj(xc, shift=mc[0], scale=mc[1], rows=bsz * lc, rows_per_seq=lc)
            y_f, y_b = ssd_scan(xs, dtv, xs_c, dtv_c, alog, bsz=bsz, seq=seq, lc=lc, tb=min(512, seq), inner=inner)
            dsk = jnp.repeat(ssd_d[k].astype(F32), SSD_HEADDIM)
            xa, aff = ssd_out(y_f, y_b, xs, z, dsk, ssd_norm_g[k], w_out, xl, ml[2], g1, ml[3], ml[4], wrt,
                              rows_per_mod=seq, tm=256, packed=packed)
        xl = _moe(xa, aff, g1, ml[3], ml[4], ml[5], w1, w3, w2, seq=seq, layer=l)
        if need_ctx:
            xc = _moe(ca, caff, g1, mc[3], mc[4], mc[5], w1, w3, w2, seq=lc, layer=l)
        packed = True
    return final_norm(xl, final_g, rows=bsz * seq, tm=512, packed=packed).reshape(bsz, seq, d)
```

```python
import functools
import math

import jax
import jax.numpy as jnp
import numpy as np
from jax import lax
from jax.experimental import pallas as pl
from jax.experimental.pallas import tpu as pltpu

F32 = jnp.float32
BF16 = jnp.bfloat16
I32 = jnp.int32

GRID_W = 64
HGRN_HEADS = 4
HGRN_DK = 128
HGRN_WIDTH = HGRN_HEADS * HGRN_DK
GLA_CHUNK = 32
NA_HEADS = 8
NA_DH = 64
NA_WIDTH = NA_HEADS * NA_DH
NA_KR = 8
NA_KC = 16
SSD_HEADDIM = 64
SSD_GROUPS = 8
SSD_STATE = 128
SSD_CONV = 5
SSD_CHUNK = 64
N_EXPERTS = 16
CAPACITY_FACTOR = 2
RMS_EPS = 1e-6
NEG_INF = -1e30

LANES = 128
SUBLANES = 8
VMEM_LIMIT_BYTES = 56 * 1024 * 1024


def _cparams(*sem):
    return pltpu.CompilerParams(dimension_semantics=sem, vmem_limit_bytes=VMEM_LIMIT_BYTES)


def _sigmoid(x):
    return 1.0 / (1.0 + jnp.exp(-x))


def _silu(x):
    return x * _sigmoid(x)


def _split3(x):
    a = x.astype(BF16)
    r = x - a.astype(F32)
    b = r.astype(BF16)
    c = (r - b.astype(F32)).astype(BF16)
    return a, b, c


def _dot(a, b):
    return jnp.dot(a, b, preferred_element_type=F32)


def _dot_nt(a, b):
    return lax.dot_general(a, b, (((1,), (1,)), ((), ())), preferred_element_type=F32)


def _dot_tn(a, b):
    return lax.dot_general(a, b, (((0,), (0,)), ((), ())), preferred_element_type=F32)


def _dot_exact_lhs(m01, x):
    m = m01.astype(BF16)
    a, b, c = _split3(x)
    return _dot(m, a) + _dot(m, b) + _dot(m, c)


def _dot3(a, b):
    a1 = a.astype(BF16)
    a2 = (a - a1.astype(F32)).astype(BF16)
    b1 = b.astype(BF16)
    b2 = (b - b1.astype(F32)).astype(BF16)
    return _dot(a1, b1) + _dot(a1, b2) + _dot(a2, b1)


def _dot3_nt(a, b):
    a1 = a.astype(BF16)
    a2 = (a - a1.astype(F32)).astype(BF16)
    b1 = b.astype(BF16)
    b2 = (b - b1.astype(F32)).astype(BF16)
    return _dot_nt(a1, b1) + _dot_nt(a1, b2) + _dot_nt(a2, b1)


def _rms_mod(x, g, shift, scale):
    y = x * lax.rsqrt(jnp.mean(x * x, axis=-1, keepdims=True) + RMS_EPS) * g
    return y * (1.0 + scale) + shift


def _row_pitch(d):
    rows = 2 * (d // LANES)
    return rows + (4 - rows % SUBLANES) % SUBLANES


def _load_packed(ref, n, part, d):
    nx = d // LANES
    pitch = _row_pitch(d)
    return jnp.concatenate([ref[pl.ds(part * nx + k, n, stride=pitch), :] for k in range(nx)], axis=1)


def _store_packed(ref, val, n, part, d):
    nx = d // LANES
    pitch = _row_pitch(d)
    for k in range(nx):
        ref[pl.ds(part * nx + k, n, stride=pitch), :] = val[:, k * LANES:(k + 1) * LANES]


def _load_stream(x_ref, packed):
    if not packed:
        return x_ref[...]
    n, d = packed
    return _load_packed(x_ref, n, 1, d)


def _stream_spec(tm, d, packed, index_map_row):
    if packed:
        return pl.BlockSpec((tm * _row_pitch(d), LANES), lambda *a: (index_map_row(*a), 0))
    return pl.BlockSpec((tm, d), lambda *a: (index_map_row(*a), 0))


def _mod_kernel(s_ref, w_ref, b_ref, o_ref):
    s = _silu(s_ref[...])
    o_ref[0] = _dot3(s, w_ref[0]) + b_ref[0]


def modulation_table(c, c_ctx, ada_w, ada_b):
    depth, d, n = ada_w.shape
    bsz = c.shape[0]
    s = jnp.zeros((SUBLANES, d), F32).at[:bsz].set(c).at[bsz].set(c_ctx)
    tn = 1024
    return pl.pallas_call(
        _mod_kernel,
        grid=(depth, n // tn),
        in_specs=[pl.BlockSpec((SUBLANES, d), lambda l, j: (0, 0)),
                  pl.BlockSpec((1, d, tn), lambda l, j: (l, 0, j)),
                  pl.BlockSpec((1, 1, tn), lambda l, j: (l, 0, j))],
        out_specs=pl.BlockSpec((1, SUBLANES, tn), lambda l, j: (l, 0, j)),
        out_shape=jax.ShapeDtypeStruct((depth, SUBLANES, n), F32),
        compiler_params=_cparams("parallel", "parallel"),
        name="mod_table",
    )(s, ada_w, ada_b.reshape(depth, 1, n))


def _proj_kernel(x_ref, g_ref, sh_ref, sc_ref, w_ref, o_ref, *, tn, packed):
    h = _rms_mod(_load_stream(x_ref, packed), g_ref[...], sh_ref[0], sc_ref[0]).astype(BF16)
    for j in range(w_ref.shape[1] // tn):
        o_ref[:, j * tn:(j + 1) * tn] = _dot(h, w_ref[:, j * tn:(j + 1) * tn])


def norm_mod_project(x, g, shift, scale, w, *, rows, rows_per_mod, tm, tn, packed):
    r = rows
    d, n = w.shape
    tm = min(tm, rows_per_mod)
    per = rows_per_mod // tm
    return pl.pallas_call(
        functools.partial(_proj_kernel, tn=tn, packed=(tm, d) if packed else None),
        grid=(r // tm,),
        in_specs=[_stream_spec(tm, d, packed, lambda i: i),
                  pl.BlockSpec((1, d), lambda i: (0, 0)),
                  pl.BlockSpec((1, 1, d), lambda i: (i // per, 0, 0)),
                  pl.BlockSpec((1, 1, d), lambda i: (i // per, 0, 0)),
                  pl.BlockSpec((d, n), lambda i: (0, 0))],
        out_specs=pl.BlockSpec((tm, n), lambda i: (i, 0)),
        out_shape=jax.ShapeDtypeStruct((r, n), F32),
        compiler_params=_cparams("parallel"),
        name="norm_mod_project",
    )(x, g.reshape(1, d), shift, scale, w)


def _tri(n, upper):
    r = lax.broadcasted_iota(I32, (n, n), 0)
    c = lax.broadcasted_iota(I32, (n, n), 1)
    return (r <= c) if upper else (r >= c)


SCAN_SUB = 256


def _block_tri(n, chunk, upper):
    r = lax.broadcasted_iota(I32, (n, n), 0)
    c = lax.broadcasted_iota(I32, (n, n), 1)
    same = (r // chunk) == (c // chunk)
    return same & ((r <= c) if upper else (r >= c))


def _hgrn_block(q, z, v, lb, st, upper):
    n = q.shape[0]
    ch = GLA_CHUNK
    nc = n // ch
    mask = _block_tri(n, ch, upper)
    qs = _silu(q)
    f = lb + (1.0 - lb) * _sigmoid(z)
    g = jnp.log(f)
    k = 1.0 - f
    c3 = _dot(mask.astype(BF16), jnp.concatenate(_split3(g), axis=1))
    dk = g.shape[1]
    cum = c3[:, 0:dk] + c3[:, dk:2 * dk] + c3[:, 2 * dk:3 * dk]
    cls = [cum[c * ch:c * ch + 1] if upper else cum[(c + 1) * ch - 1:(c + 1) * ch] for c in range(nc)]
    clb = jnp.concatenate([jnp.broadcast_to(cl, (ch, dk)) for cl in cls], axis=0)
    q_in = (qs * jnp.exp(cum)).astype(BF16)
    k_in = (k * jnp.exp(-cum)).astype(BF16)
    k_out = (k * jnp.exp(clb - cum)).astype(BF16)
    vb = v.astype(BF16)
    attn = jnp.where(mask, _dot_nt(q_in, k_in), 0.0)
    intra = _dot(attn.astype(BF16), vb)
    dv = v.shape[1]
    vt = v.T
    chunk_of_col = lax.broadcasted_iota(I32, (1, n), 1) // ch
    vexp_t = jnp.concatenate([jnp.where(chunk_of_col == c, vt, 0.0) for c in range(nc)], axis=0).astype(BF16)
    upd = _dot(vexp_t, k_out)
    sts = [None] * nc
    for c in (reversed(range(nc)) if upper else range(nc)):
        sts[c] = st
        st = st * jnp.exp(cls[c]) + upd[c * dv:(c + 1) * dv]
    chunk_of_row = lax.broadcasted_iota(I32, (n, 1), 0) // ch
    q_exp = jnp.concatenate([jnp.where(chunk_of_row == c, q_in, jnp.zeros_like(q_in)) for c in range(nc)], axis=1)
    st_cat = jnp.concatenate(sts, axis=1).astype(BF16)
    return intra + _dot_nt(q_exp, st_cat), st


def _hgrn_kernel(qf_ref, zf_ref, vf_ref, qb_ref, zb_ref, vb_ref,
                 qc_ref, zfc_ref, zbc_ref, vc_ref, lb_ref,
                 of_ref, ob_ref, ocf_ref, ocb_ref, stf_ref, stb_ref):
    lbf = lb_ref[0:1]
    lbb = lb_ref[1:2]

    def run(q_ref, z_ref, v_ref, lb, o_ref, st, upper):
        n = q_ref.shape[0]
        sub = min(SCAN_SUB, n)
        order = range(n // sub)
        for sb in (reversed(order) if upper else order):
            rs = slice(sb * sub, (sb + 1) * sub)
            o_ref[rs, :], st = _hgrn_block(q_ref[rs, :], z_ref[rs, :], v_ref[rs, :], lb, st, upper)
        return st

    @pl.when(pl.program_id(1) == 0)
    def _():
        z0 = jnp.zeros((HGRN_DK, HGRN_DK), F32)
        stf_ref[...] = run(qc_ref, zfc_ref, vc_ref, lbf, ocf_ref, z0, False)
        stb_ref[...] = run(qc_ref, zbc_ref, vc_ref, lbb, ocb_ref, z0, True)

    stf_ref[...] = run(qf_ref, zf_ref, vf_ref, lbf, of_ref, stf_ref[...], False)
    stb_ref[...] = run(qb_ref, zb_ref, vb_ref, lbb, ob_ref, stb_ref[...], True)


def hgrn_scan(p_lat, p_ctx, lb, *, bsz, seq, lc, tb):
    h = HGRN_HEADS
    dk = HGRN_DK
    nb = seq // tb
    cq, czf, czb, cv = 0, h, 2 * h, 3 * h

    def lat(col, rev):
        if rev:
            return pl.BlockSpec((tb, dk), lambda bh, i: ((bh // h) * nb + nb - 1 - i, col + bh % h))
        return pl.BlockSpec((tb, dk), lambda bh, i: ((bh // h) * nb + i, col + bh % h))

    def ctx(col):
        return pl.BlockSpec((lc, dk), lambda bh, i: (bh // h, col + bh % h))

    def out_lat(rev):
        if rev:
            return pl.BlockSpec((tb, dk), lambda bh, i: ((bh // h) * nb + nb - 1 - i, bh % h))
        return pl.BlockSpec((tb, dk), lambda bh, i: ((bh // h) * nb + i, bh % h))

    out_ctx = pl.BlockSpec((lc, dk), lambda bh, i: (bh // h, bh % h))
    w = h * dk
    return pl.pallas_call(
        _hgrn_kernel,
        grid=(bsz * h, nb),
        in_specs=[lat(cq, False), lat(czf, False), lat(cv, False),
                  lat(cq, True), lat(czb, True), lat(cv, True),
                  ctx(cq), ctx(czf), ctx(czb), ctx(cv),
                  pl.BlockSpec((2, dk), lambda bh, i: (0, bh % h))],
        out_specs=[out_lat(False), out_lat(True), out_ctx, out_ctx],
        out_shape=[jax.ShapeDtypeStruct((bsz * seq, w), F32), jax.ShapeDtypeStruct((bsz * seq, w), F32),
                   jax.ShapeDtypeStruct((bsz * lc, w), F32), jax.ShapeDtypeStruct((bsz * lc, w), F32)],
        scratch_shapes=[pltpu.VMEM((dk, dk), F32), pltpu.VMEM((dk, dk), F32)],
        compiler_params=_cparams("parallel", "arbitrary"),
        name="hgrn_scan",
    )(p_lat, p_lat, p_lat, p_lat, p_lat, p_lat, p_ctx, p_ctx, p_ctx, p_ctx, lb)


NA_RB = 8


def na_bias_table(rpb):
    w = GRID_W
    qc = np.arange(w)
    win0 = np.clip(qc - NA_KC // 2, 0, w - NA_KC)
    kc = np.arange(w)
    ok = (kc[None, :] >= win0[:, None]) & (kc[None, :] < win0[:, None] + NA_KC)
    dc = np.clip(kc[None, :] - qc[:, None] + NA_KC - 1, 0, 2 * NA_KC - 2)
    onehot = (np.arange(2 * NA_KC - 1)[:, None, None] == dc[None]).astype(np.float32)
    toep = jnp.einsum('hdm,mqc->hdqc', rpb.astype(F32), onehot, precision=lax.Precision.HIGHEST)
    toep = jnp.where(ok[None, None], toep, NEG_INF)
    t = jnp.stack([toep[:, NA_KR - 1 - v:2 * NA_KR - 1 - v] for v in range(NA_KR)], axis=1)
    t = jnp.transpose(t, (0, 1, 3, 2, 4))
    return t.reshape(rpb.shape[0], NA_KR, w, NA_KR * w)


def _na_kernel(q_ref, kp_ref, kc_ref, kn_ref, vp_ref, vc_ref, vn_ref, kx_ref, vx_ref, bias_ref,
               o_ref, ks_ref, vs_ref, *, rows):
    w = GRID_W
    blk = NA_RB * w
    i = pl.program_id(2)
    ks_ref[0:blk, :] = kp_ref[...].astype(BF16)
    ks_ref[blk:2 * blk, :] = kc_ref[...].astype(BF16)
    ks_ref[2 * blk:3 * blk, :] = kn_ref[...].astype(BF16)
    vs_ref[0:blk, :] = vp_ref[...].astype(BF16)
    vs_ref[blk:2 * blk, :] = vc_ref[...].astype(BF16)
    vs_ref[2 * blk:3 * blk, :] = vn_ref[...].astype(BF16)
    scale = NA_DH ** -0.5
    first_head = lax.broadcasted_iota(I32, (w, 2 * NA_DH), 1) < NA_DH
    qbd = []
    for rr in range(NA_RB):
        q2 = q_ref[rr * w:(rr + 1) * w, :] * scale
        qbd.append(jnp.concatenate([jnp.where(first_head, q2, 0.0), jnp.where(first_head, 0.0, q2)],
                                   axis=0).astype(BF16))
    sx = _dot_nt(jnp.concatenate(qbd, axis=0), kx_ref[...].astype(BF16))
    mx = jnp.max(sx, axis=-1, keepdims=True)
    px = jnp.exp(sx - mx)
    dx = jnp.sum(px, axis=-1, keepdims=True)
    ox = _dot(px.astype(BF16), vx_ref[...].astype(BF16))
    for rr in range(NA_RB):
        r = i * NA_RB + rr
        r0 = jnp.clip(r - NA_KR // 2, 0, rows - NA_KR)
        start = pl.multiple_of((r0 - (i - 1) * NA_RB) * w, w)
        vv = r - r0
        s = _dot_nt(qbd[rr], ks_ref[pl.ds(start, NA_KR * w), :])
        s = s + jnp.concatenate([bias_ref[0, vv], bias_ref[1, vv]], axis=0)
        mw = jnp.max(s, axis=-1, keepdims=True)
        p = jnp.exp(s - mw)
        dw = jnp.sum(p, axis=-1, keepdims=True)
        ow = _dot(p.astype(BF16), vs_ref[pl.ds(start, NA_KR * w), :])
        rs = slice(rr * 2 * w, (rr + 1) * 2 * w)
        m = jnp.maximum(mw, mx[rs])
        aw = jnp.exp(mw - m)
        ax = jnp.exp(mx[rs] - m)
        o = (ow * aw + ox[rs] * ax) / (dw * aw + dx[rs] * ax)
        o_ref[rr * w:(rr + 1) * w, :] = jnp.where(first_head, o[0:w], o[w:2 * w]).astype(o_ref.dtype)


def neighborhood_attention(p_lat, p_ctx, bias_tbl, *, bsz, seq, lc, col0):
    w = GRID_W
    rows = seq // w
    blk = NA_RB * w
    nb = rows // NA_RB
    hp = NA_HEADS // 2
    pw = 2 * NA_DH
    cq = col0 // pw
    ck = cq + hp
    cv = ck + hp

    def lat(col, off):
        def im(b, h, i):
            return (b * nb + jnp.clip(i + off, 0, nb - 1), col + h)
        return pl.BlockSpec((blk, pw), im)

    def ctx(col):
        return pl.BlockSpec((lc, pw), lambda b, h, i: (b, col + h))

    return pl.pallas_call(
        functools.partial(_na_kernel, rows=rows),
        grid=(bsz, hp, nb),
        in_specs=[lat(cq, 0), lat(ck, -1), lat(ck, 0), lat(ck, 1), lat(cv, -1), lat(cv, 0), lat(cv, 1),
                  ctx(ck), ctx(cv),
                  pl.BlockSpec((2, NA_KR, w, NA_KR * w), lambda b, h, i: (h, 0, 0, 0))],
        out_specs=pl.BlockSpec((blk, pw), lambda b, h, i: (b * nb + i, h)),
        out_shape=jax.ShapeDtypeStruct((bsz * seq, NA_WIDTH), BF16),
        scratch_shapes=[pltpu.VMEM((3 * blk, pw), BF16), pltpu.VMEM((3 * blk, pw), BF16)],
        compiler_params=_cparams("parallel", "parallel", "arbitrary"),
        name="neighborhood_attention",
    )(p_lat, p_lat, p_lat, p_lat, p_lat, p_lat, p_lat, p_ctx, p_ctx, bias_tbl)


def _ctx_attn_kernel(q_ref, k_ref, v_ref, o_ref):
    scale = NA_DH ** -0.5
    outs = []
    for hh in range(2):
        cs = slice(hh * NA_DH, (hh + 1) * NA_DH)
        q = (q_ref[:, cs] * scale).astype(BF16)
        s = _dot_nt(q, k_ref[:, cs].astype(BF16))
        p = jnp.exp(s - jnp.max(s, axis=-1, keepdims=True))
        o = _dot(p.astype(BF16), v_ref[:, cs].astype(BF16))
        outs.append(o / jnp.sum(p, axis=-1, keepdims=True))
    o_ref[...] = jnp.concatenate(outs, axis=-1).astype(o_ref.dtype)


def context_attention(p_ctx, *, bsz, lc, col0):
    hp = NA_HEADS // 2
    pw = 2 * NA_DH
    cq = col0 // pw

    def spec(col):
        return pl.BlockSpec((lc, pw), lambda b, h: (b, col + h))

    return pl.pallas_call(
        _ctx_attn_kernel,
        grid=(bsz, hp),
        in_specs=[spec(cq), spec(cq + hp), spec(cq + 2 * hp)],
        out_specs=pl.BlockSpec((lc, pw), lambda b, h: (b, h)),
        out_shape=jax.ShapeDtypeStruct((bsz * lc, NA_WIDTH), BF16),
        compiler_params=_cparams("parallel", "parallel"),
        name="context_attention",
    )(p_ctx, p_ctx, p_ctx)


def _residual_router(x, y, gate, g1, sh, sc, wrt, xa_ref, aff_ref):
    x1 = x + gate * y
    n, d = x1.shape
    _store_packed(xa_ref, x1, n, 0, d)
    _store_packed(xa_ref, x1, n, 1, d)
    pitch = _row_pitch(d)
    for k in range(2 * (d // LANES), pitch):
        xa_ref[pl.ds(k, n, stride=pitch), :] = jnp.zeros((n, LANES), F32)
    f = _rms_mod(x1, g1, sh, sc)
    lt = _dot3_nt(wrt, f)
    e = jnp.exp(lt - jnp.max(lt, axis=0, keepdims=True))
    aff_ref[0] = e / jnp.sum(e, axis=0, keepdims=True)


def _ab_out_kernel(of_ref, ob_ref, gg_ref, nl_ref, on_ref, w_ref, x_ref, gate_ref, g1_ref, sh_ref, sc_ref, wrt_ref,
                   xa_ref, aff_ref, *, packed):
    o = of_ref[...] + ob_ref[...]
    gain = on_ref[...]
    parts = []
    for h in range(HGRN_HEADS):
        oh = o[:, h * HGRN_DK:(h + 1) * HGRN_DK]
        parts.append(oh * lax.rsqrt(jnp.mean(oh * oh, axis=-1, keepdims=True) + RMS_EPS) * gain)
    hl = (jnp.concatenate(parts, axis=-1) * _silu(gg_ref[...])).astype(BF16)
    y = _dot(hl, w_ref[0:HGRN_WIDTH, :]) + _dot(nl_ref[...], w_ref[HGRN_WIDTH:, :])
    _residual_router(_load_stream(x_ref, packed), y, gate_ref[0], g1_ref[...], sh_ref[0], sc_ref[0], wrt_ref[...],
                     xa_ref, aff_ref)


def _ssd_out_kernel(yf_ref, yb_ref, xs_ref, z_ref, dsk_ref, ng_ref, w_ref, x_ref, gate_ref, g1_ref, sh_ref, sc_ref,
                    wrt_ref, xa_ref, aff_ref, *, group_width, packed):
    y = (yf_ref[...] + yb_ref[...] + dsk_ref[...] * xs_ref[...]) * _silu(z_ref[...])
    parts = []
    for g in range(y.shape[1] // group_width):
        yg = y[:, g * group_width:(g + 1) * group_width]
        parts.append(yg * lax.rsqrt(jnp.mean(yg * yg, axis=-1, keepdims=True) + RMS_EPS))
    yn = (jnp.concatenate(parts, axis=-1) * ng_ref[...]).astype(BF16)
    _residual_router(_load_stream(x_ref, packed), _dot(yn, w_ref[...]), gate_ref[0], g1_ref[...], sh_ref[0],
                     sc_ref[0], wrt_ref[...], xa_ref, aff_ref)


def _out_common(x, gate, g1, sh, sc, wrt, *, rows, rows_per_mod, tm, packed):
    r = rows
    e, d = wrt.shape
    per = rows_per_mod // tm
    nbatch = r // rows_per_mod
    pitch = _row_pitch(d)
    mod = lambda i: (i // per, 0, 0)
    in_specs = [_stream_spec(tm, d, packed, lambda i: i), pl.BlockSpec((1, 1, d), mod),
                pl.BlockSpec((1, d), lambda i: (0, 0)),
                pl.BlockSpec((1, 1, d), mod), pl.BlockSpec((1, 1, d), mod), pl.BlockSpec((e, d), lambda i: (0, 0))]
    out_specs = [pl.BlockSpec((tm * pitch, LANES), lambda i: (i, 0)),
                 pl.BlockSpec((1, e, tm), lambda i: (i // per, 0, i % per))]
    out_shape = [jax.ShapeDtypeStruct((r * pitch, LANES), F32), jax.ShapeDtypeStruct((nbatch, e, rows_per_mod), F32)]
    args = (x, gate, g1.reshape(1, d), sh, sc, wrt)
    return in_specs, out_specs, out_shape, args


def ab_out(o_f, o_b, p, nl, onorm_g, w_out, x, gate, g1, sh, sc, wrt, *, rows_per_mod, tm, packed):
    tm = min(tm, rows_per_mod)
    r = o_f.shape[0]
    wd = HGRN_WIDTH
    row = lambda i: (i, 0)
    d = wrt.shape[1]
    c_in, c_out, c_shape, c_args = _out_common(x, gate, g1, sh, sc, wrt, rows=r, rows_per_mod=rows_per_mod, tm=tm,
                                               packed=packed)
    in_specs = [pl.BlockSpec((tm, wd), row), pl.BlockSpec((tm, wd), row),
                pl.BlockSpec((tm, wd), lambda i: (i, 4)),
                pl.BlockSpec((tm, NA_WIDTH), row),
                pl.BlockSpec((1, HGRN_DK), lambda i: (0, 0)),
                pl.BlockSpec(w_out.shape, lambda i: (0, 0))] + c_in
    return pl.pallas_call(
        functools.partial(_ab_out_kernel, packed=(tm, d) if packed else None),
        grid=(r // tm,), in_specs=in_specs, out_specs=c_out, out_shape=c_shape,
        compiler_params=_cparams("parallel"), name="ab_out",
    )(o_f, o_b, p, nl, onorm_g.reshape(1, HGRN_DK), w_out, *c_args)


def ssd_out(y_f, y_b, xa, p, dsk, ng, w_out, x, gate, g1, sh, sc, wrt, *, rows_per_mod, tm, packed):
    tm = min(tm, rows_per_mod)
    r = y_f.shape[0]
    inner = w_out.shape[0]
    row = lambda i: (i, 0)
    d = wrt.shape[1]
    c_in, c_out, c_shape, c_args = _out_common(x, gate, g1, sh, sc, wrt, rows=r, rows_per_mod=rows_per_mod, tm=tm,
                                               packed=packed)
    in_specs = [pl.BlockSpec((tm, inner), row), pl.BlockSpec((tm, inner), row),
                pl.BlockSpec((tm, inner), row),
                pl.BlockSpec((tm, inner), row),
                pl.BlockSpec((1, inner), lambda i: (0, 0)), pl.BlockSpec((1, inner), lambda i: (0, 0)),
                pl.BlockSpec(w_out.shape, lambda i: (0, 0))] + c_in
    return pl.pallas_call(
        functools.partial(_ssd_out_kernel, group_width=inner // SSD_GROUPS, packed=(tm, d) if packed else None),
        grid=(r // tm,), in_specs=in_specs, out_specs=c_out, out_shape=c_shape,
        compiler_params=_cparams("parallel"), name="ssd_out",
    )(y_f, y_b, xa, p, dsk.reshape(1, inner), ng.reshape(1, inner), w_out, *c_args)


F32_INF_BITS = 0x7F800000


def _lane_cumsum(src_ref, dst_ref, n_chunks):
    e = src_ref.shape[0]
    upper = _tri(LANES, True).astype(BF16)

    def body(j, off):
        ds = pl.ds(pl.multiple_of(j * LANES, LANES), LANES)
        c = _dot(src_ref[:, ds].astype(BF16), upper) + off
        dst_ref[:, ds] = c
        return c[:, LANES - 1:LANES]

    lax.fori_loop(0, n_chunks, body, jnp.zeros((e, 1), F32), unroll=math.gcd(n_chunks, 8))


def _select_kernel(aff_ref, idx_ref, gate_ref, m_scr, c_scr, *, seq, cap, nph, kc, row_scale):
    e = aff_ref.shape[1]
    n_chunks = seq // LANES
    bits = pltpu.bitcast(aff_ref[0], I32)

    def bisect(_, lohi):
        lo, hi = lohi
        mid = lo + lax.shift_right_logical(hi - lo, 1)
        cnt = jnp.sum((bits >= mid).astype(F32), axis=1, keepdims=True)
        ge = cnt >= cap
        return jnp.where(ge, mid, lo), jnp.where(ge, hi, mid)

    thr, _ = lax.fori_loop(0, 31, bisect, (jnp.zeros((e, 1), I32), jnp.full((e, 1), F32_INF_BITS, I32)))
    gt = bits > thr
    eq = bits == thr
    need = cap - jnp.sum(gt.astype(F32), axis=1, keepdims=True)
    m_scr[...] = eq.astype(F32)
    _lane_cumsum(m_scr, c_scr, n_chunks)
    sel = gt | (eq & (c_scr[...] <= need))
    m_scr[...] = sel.astype(F32)
    _lane_cumsum(m_scr, c_scr, n_chunks)

    sub_p = lax.broadcasted_iota(I32, (nph, kc), 0).astype(F32)
    sub_l = lax.broadcasted_iota(I32, (LANES, kc), 0).astype(F32)
    lane_t = lax.broadcasted_iota(I32, (1, kc), 1)
    inv = 1.0 / LANES
    for ex in range(e):
        def body(c, acc):
            ds = pl.ds(pl.multiple_of(c * kc, LANES), kc)
            pos = c_scr[ex:ex + 1, ds] - 1.0
            phi = jnp.floor(pos * inv)
            plo = pos - LANES * phi
            hit = (phi == sub_p) & (m_scr[ex:ex + 1, ds] > 0.0)
            t = (lane_t + c * kc).astype(F32)
            th = jnp.floor(t * inv)
            tl = t - LANES * th
            a1, a2, a3 = _split3(aff_ref[0, ex:ex + 1, ds])
            rows = [th, tl, a1.astype(F32), a2.astype(F32), a3.astype(F32)]
            lhs = jnp.concatenate([jnp.where(hit, r, 0.0) for r in rows], axis=0).astype(BF16)
            onehot = (plo == sub_l).astype(BF16)
            return acc + _dot_nt(lhs, onehot)

        acc = lax.fori_loop(0, seq // kc, body, jnp.zeros((5 * nph, LANES), F32), unroll=math.gcd(seq // kc, 2))
        token = (acc[0:nph] * LANES + acc[nph:2 * nph]).astype(I32)
        idx_ref[0, ex] = (pl.program_id(0) * seq + token) * row_scale
        gate_ref[0, ex] = acc[2 * nph:3 * nph] + acc[3 * nph:4 * nph] + acc[4 * nph:5 * nph]


def expert_select(aff_t, cap, row_scale):
    bsz, e, seq = aff_t.shape
    nph = -(-cap // LANES)
    nph = -(-nph // SUBLANES) * SUBLANES
    kc = min(2048, seq)
    idx, gate = pl.pallas_call(
        functools.partial(_select_kernel, seq=seq, cap=cap, nph=nph, kc=kc, row_scale=row_scale),
        grid=(bsz,),
        in_specs=[pl.BlockSpec((1, e, seq), lambda b: (b, 0, 0))],
        out_specs=[pl.BlockSpec((1, e, nph, LANES), lambda b: (b, 0, 0, 0)),
                   pl.BlockSpec((1, e, nph, LANES), lambda b: (b, 0, 0, 0))],
        out_shape=[jax.ShapeDtypeStruct((bsz, e, nph, LANES), I32),
                   jax.ShapeDtypeStruct((bsz, e, nph, LANES), F32)],
        scratch_shapes=[pltpu.VMEM((e, seq), F32), pltpu.VMEM((e, seq), F32)],
        compiler_params=_cparams("parallel"),
        name="expert_select",
    )(aff_t)
    return idx.reshape(bsz, e, nph * LANES)[:, :, :cap], gate.reshape(bsz, e, nph * LANES)[:, :, :cap]


FFN_SLOTS = 3
FFN_TILE = 512


def _ffn_kernel(idx_ref, gate_ref, g1_ref, sh_ref, sc_ref, g5_ref, w1_ref, w3_ref, w2_ref, xa_in,
                xa_hbm, buf, gsem, ssem, *, tr, nt, d):
    del xa_in
    nx = d // LANES
    pitch = _row_pitch(d)
    grp = min(8, tr)
    j = pl.program_id(2)
    step = (pl.program_id(0) * pl.num_programs(1) + pl.program_id(1)) * nt + j
    slot = step % FFN_SLOTS
    nslot = (step + 1) % FFN_SLOTS

    def start_gathers(tile, sl):
        def body(i, c):
            for k in range(grp):
                src = idx_ref[0, 0, tile * tr + i * grp + k]
                dst = i * (grp * pitch) + k * pitch
                pltpu.make_async_copy(xa_hbm.at[pl.ds(src, 2 * nx)], buf.at[sl, pl.ds(dst, 2 * nx)],
                                      gsem.at[sl]).start()
            return c
        lax.fori_loop(0, tr // grp, body, 0)

    def start_scatters(tile, sl):
        def body(i, c):
            for k in range(grp):
                dst = idx_ref[0, 0, tile * tr + i * grp + k] + nx
                src = i * (grp * pitch) + k * pitch + nx
                pltpu.make_async_copy(buf.at[sl, pl.ds(src, nx)], xa_hbm.at[pl.ds(dst, nx)], ssem.at[sl]).start()
            return c
        lax.fori_loop(0, tr // grp, body, 0)

    def wait_gathers(sl):
        n = tr * 2 * nx
        pltpu.make_async_copy(xa_hbm.at[pl.ds(0, n)], buf.at[sl, pl.ds(0, n)], gsem.at[sl]).wait()

    def wait_scatters(sl):
        n = tr * nx
        pltpu.make_async_copy(buf.at[sl, pl.ds(0, n)], xa_hbm.at[pl.ds(0, n)], ssem.at[sl]).wait()

    @pl.when(j == 0)
    def _():
        start_gathers(0, slot)

    @pl.when(j + 1 < nt)
    def _():
        @pl.when(j >= FFN_SLOTS - 1)
        def _():
            wait_scatters(nslot)

        start_gathers(j + 1, nslot)

    wait_gathers(slot)
    rows = buf.at[slot]
    f = _rms_mod(_load_packed(rows, tr, 0, d), g1_ref[...], sh_ref[0], sc_ref[0]).astype(BF16)
    hid = (_silu(_dot(f, w1_ref[0, 0])) * _dot(f, w3_ref[0, 0])).astype(BF16)
    y = _dot(hid, w2_ref[0, 0])
    eye = lax.broadcasted_iota(I32, (tr, tr), 0) == lax.broadcasted_iota(I32, (tr, tr), 1)
    gcol = jnp.sum(jnp.where(eye, gate_ref[0], 0.0), axis=1, keepdims=True)
    _store_packed(rows, _load_packed(rows, tr, 1, d) + (g5_ref[0] * gcol) * y, tr, 1, d)
    start_scatters(j, slot)

    @pl.when(j == nt - 1)
    def _():
        for back in range(min(FFN_SLOTS, nt)):
            wait_scatters((step - back) % FFN_SLOTS)


def expert_ffn_combine(xa, idx, gate, g1, sh, sc, g5, w1, w3, w2, *, layer):
    bsz, e, cap = idx.shape
    d, ff = w1.shape[2], w1.shape[3]
    tr = min(FFN_TILE, cap)
    nt = cap // tr
    mod = lambda b, ex, j: (b, 0, 0)
    return pl.pallas_call(
        functools.partial(_ffn_kernel, tr=tr, nt=nt, d=d),
        grid=(bsz, e, nt),
        in_specs=[pl.BlockSpec((1, 1, cap), lambda b, ex, j: (b * e + ex, 0, 0), memory_space=pltpu.SMEM),
                  pl.BlockSpec((1, 1, tr), lambda b, ex, j: ((b * e + ex) * nt + j, 0, 0)),
                  pl.BlockSpec((1, d), lambda b, ex, j: (0, 0)),
                  pl.BlockSpec((1, 1, d), mod), pl.BlockSpec((1, 1, d), mod), pl.BlockSpec((1, 1, d), mod),
                  pl.BlockSpec((1, 1, d, ff), lambda b, ex, j: (layer, ex, 0, 0)),
                  pl.BlockSpec((1, 1, d, ff), lambda b, ex, j: (layer, ex, 0, 0)),
                  pl.BlockSpec((1, 1, ff, d), lambda b, ex, j: (layer, ex, 0, 0)),
                  pl.BlockSpec(memory_space=pl.ANY)],
        out_specs=pl.BlockSpec(memory_space=pl.ANY),
        out_shape=jax.ShapeDtypeStruct(xa.shape, F32),
        scratch_shapes=[pltpu.VMEM((FFN_SLOTS, tr * _row_pitch(d), LANES), F32),
                        pltpu.SemaphoreType.DMA((FFN_SLOTS,)), pltpu.SemaphoreType.DMA((FFN_SLOTS,))],
        input_output_aliases={9: 0},
        compiler_params=_cparams("arbitrary", "arbitrary", "arbitrary"),
        name="expert_ffn_combine",
    )(idx.reshape(bsz * e, 1, cap), gate.reshape(bsz * e * nt, 1, tr), g1.reshape(1, d), sh, sc, g5,
      w1, w3, w2, xa)


CONV_HALO = 8


def _ssd_in_kernel(cur_ref, prev_ref, next_ref, g_ref, sh_ref, sc_ref, w_ref, cw_ref, cb_ref, dtb_ref,
                   z_ref, xa_ref, dtv_ref, ext_ref, *, tm, per, inner, cdim, packed):
    i = pl.program_id(0)
    first = (i % per) == 0
    last = (i % per) == per - 1
    halo = (CONV_HALO, packed[1]) if packed else None

    def normed(ref, pk):
        return _rms_mod(_load_stream(ref, pk), g_ref[...], sh_ref[0], sc_ref[0]).astype(BF16)

    h = normed(cur_ref, packed)
    hp = normed(prev_ref, halo)
    hn = normed(next_ref, halo)
    z_ref[...] = _dot(h, w_ref[:, 0:inner])
    v = _dot(h, w_ref[:, inner + cdim:inner + cdim + LANES]) + dtb_ref[...]
    dtv_ref[...] = jnp.maximum(v, 0.0) + jnp.log1p(jnp.exp(-jnp.abs(v)))
    half = cdim // 2
    lo = CONV_HALO - SSD_CONV // 2
    for j in range(2):
        cols = slice(inner + j * half, inner + (j + 1) * half)
        out = slice(j * half, (j + 1) * half)
        ext_ref[0:CONV_HALO, :] = jnp.where(first, 0.0, _dot(hp, w_ref[:, cols]))
        ext_ref[CONV_HALO:CONV_HALO + tm, :] = _dot(h, w_ref[:, cols])
        ext_ref[CONV_HALO + tm:2 * CONV_HALO + tm, :] = jnp.where(last, 0.0, _dot(hn, w_ref[:, cols]))
        acc = cb_ref[:, out] + cw_ref[0:1, out] * ext_ref[pl.ds(lo, tm), :]
        for k in range(1, SSD_CONV):
            acc = acc + cw_ref[k:k + 1, out] * ext_ref[pl.ds(lo + k, tm), :]
        xa_ref[:, out] = _silu(acc)


def ssd_in(x, g, shift, scale, w, conv_w, conv_b, dt_bias, *, rows, rows_per_seq, tm, inner, packed):
    r = rows
    d = w.shape[0]
    cdim = conv_w.shape[1]
    tm = min(tm, rows_per_seq)
    per = rows_per_seq // tm
    hb = tm // CONV_HALO
    nhb = r // CONV_HALO
    const = lambda i: (0, 0)
    mod = lambda i: (i // per, 0, 0)
    return pl.pallas_call(
        functools.partial(_ssd_in_kernel, tm=tm, per=per, inner=inner, cdim=cdim,
                          packed=(tm, d) if packed else None),
        grid=(r // tm,),
        in_specs=[_stream_spec(tm, d, packed, lambda i: i),
                  _stream_spec(CONV_HALO, d, packed, lambda i: jnp.maximum(i * hb - 1, 0)),
                  _stream_spec(CONV_HALO, d, packed, lambda i: jnp.minimum((i + 1) * hb, nhb - 1)),
                  pl.BlockSpec((1, d), const), pl.BlockSpec((1, 1, d), mod), pl.BlockSpec((1, 1, d), mod),
                  pl.BlockSpec(w.shape, const), pl.BlockSpec((SSD_CONV, cdim), const),
                  pl.BlockSpec((1, cdim), const), pl.BlockSpec((1, LANES), const)],
        out_specs=[pl.BlockSpec((tm, inner), lambda i: (i, 0)), pl.BlockSpec((tm, cdim), lambda i: (i, 0)),
                   pl.BlockSpec((tm, LANES), lambda i: (i, 0))],
        out_shape=[jax.ShapeDtypeStruct((r, inner), F32), jax.ShapeDtypeStruct((r, cdim), F32),
                   jax.ShapeDtypeStruct((r, LANES), F32)],
        scratch_shapes=[pltpu.VMEM((tm + 2 * CONV_HALO, cdim // 2), F32)],
        compiler_params=_cparams("parallel"),
        name="ssd_in",
    )(x, x, x, g.reshape(1, d), shift, scale, w, conv_w, conv_b.reshape(1, cdim), dt_bias)


SSD_HPG = 4


def _ssd_block(x, bm, cm, dtv, a_row, hs, lane0, upper, need_y=True):
    t = x.shape[0]
    hp = x.shape[1]
    pdim = hp // SSD_HPG
    ch = SSD_CHUNK
    nc = t // ch
    lane = lax.broadcasted_iota(I32, (1, LANES), 1)
    head_of_lane = lax.broadcasted_iota(I32, (1, hp), 1) // pdim
    dtx = jnp.zeros((t, hp), F32)
    a_x = jnp.zeros((1, hp), F32)
    for h in range(SSD_HPG):
        pick = lane == lane0 + h
        dtx = jnp.where(head_of_lane == h, jnp.sum(jnp.where(pick, dtv, 0.0), axis=1, keepdims=True), dtx)
        a_x = jnp.where(head_of_lane == h, jnp.sum(jnp.where(pick, a_row, 0.0), axis=1, keepdims=True), a_x)
    dtax = dtx * a_x
    c3 = _dot(_block_tri(t, ch, upper).astype(BF16), jnp.concatenate(_split3(dtax), axis=1))
    acx = c3[:, 0:hp] + c3[:, hp:2 * hp] + c3[:, 2 * hp:3 * hp]
    r = lax.broadcasted_iota(I32, (ch, hp), 0)
    s = lax.broadcasted_iota(I32, (ch, hp), 1) % ch
    src_le = (s <= r) if upper else (s >= r)
    ident = s == r
    causal = (r <= s) if upper else (r >= s)
    same_head = (lax.broadcasted_iota(I32, (hp, hp), 0) // pdim) == (lax.broadcasted_iota(I32, (hp, hp), 1) // pdim)
    ydiag, decay_out, alasts, upds = [], [], [], []
    for c in range(nc):
        rs = slice(c * ch, (c + 1) * ch)
        acx_c, dtx_c, x_c = acx[rs], dtx[rs], x[rs]
        bm_c = bm[rs].astype(BF16)
        alast = acx_c[0:1] if upper else acx_c[ch - 1:ch]
        if need_y:
            arx = jnp.sum(jnp.where(src_le, dtax[rs], 0.0), axis=0, keepdims=True)
            dtr = jnp.sum(jnp.where(ident, dtx_c, 0.0), axis=0, keepdims=True)
            dec = jnp.where(causal, jnp.exp(jnp.minimum(acx_c - arx, 0.0)), 0.0)
            cbx = _dot_nt(cm[rs].astype(BF16), jnp.concatenate([bm_c] * SSD_HPG, axis=0))
            gx = (cbx * dec * dtr).astype(BF16)
            xbd = jnp.where(same_head, jnp.concatenate([x_c] * SSD_HPG, axis=0), 0.0).astype(BF16)
            ydiag.append(_dot(gx, xbd))
            decay_out.append(jnp.exp(acx_c))
        alasts.append(alast)
        upds.append(_dot_tn(bm_c, (x_c * (jnp.exp(alast - acx_c) * dtx_c)).astype(BF16)))
    ys = [None] * nc
    for c in (reversed(range(nc)) if upper else range(nc)):
        if need_y:
            ys[c] = ydiag[c] + _dot(cm[c * ch:(c + 1) * ch].astype(BF16), hs.astype(BF16)) * decay_out[c]
        hs = hs * jnp.exp(alasts[c]) + upds[c]
    return (jnp.concatenate(ys, axis=0) if need_y else None), hs


def _ssd_kernel(xf_ref, bf_ref, cf_ref, df_ref, xb_ref, bb_ref, cb_ref, db_ref,
                xc_ref, bc_ref, cc_ref, dc_ref, alog_ref, yf_ref, yb_ref, hf_ref, hb_ref):
    g = pl.program_id(0) % SSD_GROUPS
    a_row = -jnp.exp(alog_ref[...])
    lane_f = g * SSD_HPG
    lane_b = SSD_GROUPS * SSD_HPG + g * SSD_HPG

    def run(x_ref, b_ref, c_ref, d_ref, y_ref, hs, lane0, upper):
        n = x_ref.shape[0]
        sub = min(SCAN_SUB, n)
        order = range(n // sub)
        for sb in (reversed(order) if upper else order):
            rs = slice(sb * sub, (sb + 1) * sub)
            y, hs = _ssd_block(x_ref[rs, :], b_ref[rs, :], c_ref[rs, :], d_ref[rs, :], a_row, hs, lane0, upper,
                               need_y=y_ref is not None)
            if y_ref is not None:
                y_ref[rs, :] = y
        return hs

    @pl.when(pl.program_id(1) == 0)
    def _():
        z0 = jnp.zeros(hf_ref.shape, F32)
        hf_ref[...] = run(xc_ref, bc_ref, cc_ref, dc_ref, None, z0, lane_f, False)
        hb_ref[...] = run(xc_ref, bc_ref, cc_ref, dc_ref, None, z0, lane_b, True)

    hf_ref[...] = run(xf_ref, bf_ref, cf_ref, df_ref, yf_ref, hf_ref[...], lane_f, False)
    hb_ref[...] = run(xb_ref, bb_ref, cb_ref, db_ref, yb_ref, hb_ref[...], lane_b, True)


def ssd_scan(xa, dtv, xa_c, dtv_c, a_log_row, *, bsz, seq, lc, tb, inner):
    g = SSD_GROUPS
    n = SSD_STATE
    xw = inner // g
    nb = seq // tb
    cb0 = inner // n
    cc0 = cb0 + g

    def blk(i, rev):
        return nb - 1 - i if rev else i

    def lat(width, col, rev):
        return pl.BlockSpec((tb, width), lambda bg, i: ((bg // g) * nb + blk(i, rev), col + bg % g))

    def lat_dt(rev):
        return pl.BlockSpec((tb, LANES), lambda bg, i: ((bg // g) * nb + blk(i, rev), 0))

    def ctx(width, col):
        return pl.BlockSpec((lc, width), lambda bg, i: (bg // g, col + bg % g))

    def out(rev):
        return pl.BlockSpec((tb, xw), lambda bg, i: ((bg // g) * nb + blk(i, rev), bg % g))

    return pl.pallas_call(
        _ssd_kernel,
        grid=(bsz * g, nb),
        in_specs=[lat(xw, 0, False), lat(n, cb0, False), lat(n, cc0, False), lat_dt(False),
                  lat(xw, 0, True), lat(n, cb0, True), lat(n, cc0, True), lat_dt(True),
                  ctx(xw, 0), ctx(n, cb0), ctx(n, cc0), pl.BlockSpec((lc, LANES), lambda bg, i: (bg // g, 0)),
                  pl.BlockSpec((1, LANES), lambda bg, i: (0, 0))],
        out_specs=[out(False), out(True)],
        out_shape=[jax.ShapeDtypeStruct((bsz * seq, inner), F32), jax.ShapeDtypeStruct((bsz * seq, inner), F32)],
        scratch_shapes=[pltpu.VMEM((n, xw), F32), pltpu.VMEM((n, xw), F32)],
        compiler_params=_cparams("parallel", "arbitrary"),
        name="ssd_scan",
    )(xa, xa, xa, dtv, xa, xa, xa, dtv, xa_c, xa_c, xa_c, dtv_c, a_log_row)


def _final_norm_kernel(x_ref, g_ref, o_ref, *, packed):
    x = _load_stream(x_ref, packed)
    o_ref[...] = x * lax.rsqrt(jnp.mean(x * x, axis=-1, keepdims=True) + RMS_EPS) * g_ref[...]


def final_norm(x, g, *, rows, tm, packed):
    r = rows
    d = g.shape[0]
    return pl.pallas_call(
        functools.partial(_final_norm_kernel, packed=(tm, d) if packed else None),
        grid=(r // tm,),
        in_specs=[_stream_spec(tm, d, packed, lambda i: i), pl.BlockSpec((1, d), lambda i: (0, 0))],
        out_specs=pl.BlockSpec((tm, d), lambda i: (i, 0)),
        out_shape=jax.ShapeDtypeStruct((r, d), F32),
        compiler_params=_cparams("parallel"),
        name="final_norm",
    )(x, g.reshape(1, d))


def _moe(xa, aff, g1, sh, sc, g5, w1, w3, w2, *, seq, layer):
    cap = max(1, CAPACITY_FACTOR * seq // N_EXPERTS)
    idx, gate = expert_select(aff, cap, _row_pitch(w1.shape[2]))
    return expert_ffn_combine(xa, idx, gate, g1, sh, sc, g5, w1, w3, w2, layer=layer)


def kernel(x, c, ctx, c_ctx, ada_w, ada_b, norm_g, final_g, ab_w_in, ab_w_out, hgrn_lb_logits, hgrn_onorm_g, na_rpb, ssd_w_in, ssd_conv_w, ssd_conv_b, ssd_a_log, ssd_dt_bias, ssd_d, ssd_norm_g, ssd_w_out, moe_router, moe_w1, moe_w3, moe_w2):
    bsz, seq, d = x.shape
    lc = ctx.shape[1]
    depth = ada_w.shape[0]
    assert depth == 2, "context outputs of an SSD layer are not implemented (only needed when a layer follows it)"
    mods = modulation_table(c, c_ctx, ada_w, ada_b)
    lb_all = jnp.cumsum(jax.nn.softmax(hgrn_lb_logits.astype(F32), axis=1), axis=1)
    xl = x.reshape(bsz * seq, d)
    xc = ctx.reshape(bsz * lc, d)
    packed = False
    w1, w3, w2 = (w.astype(BF16) for w in (moe_w1, moe_w3, moe_w2))
    hg = 5 * HGRN_WIDTH
    for l in range(depth):
        need_ctx = l < depth - 1
        k = l // 2
        m = mods[l]
        ml = [m[:bsz, j * d:(j + 1) * d].reshape(bsz, 1, d) for j in range(6)]
        mc = [jnp.broadcast_to(m[bsz, j * d:(j + 1) * d].reshape(1, 1, d), (bsz, 1, d)) for j in range(6)]
        wrt = moe_router[l].T
        g0, g1 = norm_g[l, 0], norm_g[l, 1]
        if l % 2 == 0:
            w_in = ab_w_in[k].astype(BF16)
            w_out = ab_w_out[k].astype(BF16)
            p_lat = norm_mod_project(xl, g0, ml[0], ml[1], w_in, rows=bsz * seq, rows_per_mod=seq, tm=512, tn=1024,
                                     packed=packed)
            p_ctx = norm_mod_project(xc, g0, mc[0], mc[1], w_in, rows=bsz * lc, rows_per_mod=lc, tm=512, tn=1024,
                                     packed=packed)
            o_f, o_b, oc_f, oc_b = hgrn_scan(p_lat, p_ctx, lb_all[:, k], bsz=bsz, seq=seq, lc=lc, tb=min(1024, seq))
            nl = neighborhood_attention(p_lat, p_ctx, na_bias_table(na_rpb[k]), bsz=bsz, seq=seq, lc=lc, col0=hg)
            xa, aff = ab_out(o_f, o_b, p_lat, nl, hgrn_onorm_g[k], w_out, xl, ml[2], g1, ml[3], ml[4], wrt,
                             rows_per_mod=seq, tm=512, packed=packed)
            if need_ctx:
                nc = context_attention(p_ctx, bsz=bsz, lc=lc, col0=hg)
                ca, caff = ab_out(oc_f, oc_b, p_ctx, nc, hgrn_onorm_g[k], w_out, xc, mc[2], g1, mc[3], mc[4],
                                  wrt, rows_per_mod=lc, tm=512, packed=packed)
        else:
            inner = ssd_w_out.shape[1]
            heads2 = ssd_dt_bias.shape[1] * ssd_dt_bias.shape[2]
            wz = ssd_w_in[k]
            pad = jnp.zeros((d, LANES - heads2), F32)
            w_in = jnp.concatenate([wz, pad], axis=1).astype(BF16)
            w_out = ssd_w_out[k].astype(BF16)
            dtb = jnp.zeros((1, LANES), F32).at[0, :heads2].set(ssd_dt_bias[k].reshape(-1))
            alog = jnp.zeros((1, LANES), F32).at[0, :heads2].set(ssd_a_log[k].reshape(-1))
            proj = functools.partial(ssd_in, g=g0, w=w_in, conv_w=ssd_conv_w[k], conv_b=ssd_conv_b[k], dt_bias=dtb,
                                     tm=256, inner=inner, packed=packed)
            z, xs, dtv = proj(xl, shift=ml[0], scale=ml[1], rows=bsz * seq, rows_per_seq=seq)
            _, xs_c, dtv_c = proj(xc, shift=mc[0], scale=mc[1], rows=bsz * lc, rows_per_seq=lc)
            y_f, y_b = ssd_scan(xs, dtv, xs_c, dtv_c, alog, bsz=bsz, seq=seq, lc=lc, tb=min(1024, seq), inner=inner)
            dsk = jnp.repeat(ssd_d[k].astype(F32), SSD_HEADDIM)
            xa, aff = ssd_out(y_f, y_b, xs, z, dsk, ssd_norm_g[k], w_out, xl, ml[2], g1, ml[3], ml[4], wrt,
                              rows_per_mod=seq, tm=256, packed=packed)
        xl = _moe(xa, aff, g1, ml[3], ml[4], ml[5], w1, w3, w2, seq=seq, layer=l)
        if need_ctx:
            xc = _moe(ca, caff, g1, mc[3], mc[4], mc[5], w1, w3, w2, seq=lc, layer=l)
        packed = True
    return final_norm(xl, final_g, rows=bsz * seq, tm=512, packed=packed).reshape(bsz, seq, d)
```

```python
import functools
import math

import jax
import jax.numpy as jnp
import numpy as np
from jax import lax
from jax.experimental import pallas as pl
from jax.experimental.pallas import tpu as pltpu

F32 = jnp.float32
BF16 = jnp.bfloat16
I32 = jnp.int32

GRID_W = 64
HGRN_HEADS = 4
HGRN_DK = 128
HGRN_WIDTH = HGRN_HEADS * HGRN_DK
GLA_CHUNK = 32
NA_HEADS = 8
NA_DH = 64
NA_WIDTH = NA_HEADS * NA_DH
NA_KR = 8
NA_KC = 16
SSD_HEADDIM = 64
SSD_GROUPS = 8
SSD_STATE = 128
SSD_CONV = 5
SSD_CHUNK = 64
N_EXPERTS = 16
CAPACITY_FACTOR = 2
RMS_EPS = 1e-6
NEG_INF = -1e30

LANES = 128
SUBLANES = 8
VMEM_LIMIT_BYTES = 56 * 1024 * 1024


def _cparams(*sem):
    return pltpu.CompilerParams(dimension_semantics=sem, vmem_limit_bytes=VMEM_LIMIT_BYTES)


def _sigmoid(x):
    return 1.0 / (1.0 + jnp.exp(-x))


def _silu(x):
    return x * _sigmoid(x)


def _split3(x):
    a = x.astype(BF16)
    r = x - a.astype(F32)
    b = r.astype(BF16)
    c = (r - b.astype(F32)).astype(BF16)
    return a, b, c


def _dot(a, b):
    return jnp.dot(a, b, preferred_element_type=F32)


def _dot_nt(a, b):
    return lax.dot_general(a, b, (((1,), (1,)), ((), ())), preferred_element_type=F32)


def _dot_tn(a, b):
    return lax.dot_general(a, b, (((0,), (0,)), ((), ())), preferred_element_type=F32)


def _dot_exact_lhs(m01, x):
    m = m01.astype(BF16)
    a, b, c = _split3(x)
    return _dot(m, a) + _dot(m, b) + _dot(m, c)


def _dot3(a, b):
    a1 = a.astype(BF16)
    a2 = (a - a1.astype(F32)).astype(BF16)
    b1 = b.astype(BF16)
    b2 = (b - b1.astype(F32)).astype(BF16)
    return _dot(a1, b1) + _dot(a1, b2) + _dot(a2, b1)


def _dot3_nt(a, b):
    a1 = a.astype(BF16)
    a2 = (a - a1.astype(F32)).astype(BF16)
    b1 = b.astype(BF16)
    b2 = (b - b1.astype(F32)).astype(BF16)
    return _dot_nt(a1, b1) + _dot_nt(a1, b2) + _dot_nt(a2, b1)


def _rms_mod(x, g, shift, scale):
    y = x * lax.rsqrt(jnp.mean(x * x, axis=-1, keepdims=True) + RMS_EPS) * g
    return y * (1.0 + scale) + shift


def _row_pitch(d):
    rows = 2 * (d // LANES)
    return rows + (4 - rows % SUBLANES) % SUBLANES


def _load_packed(ref, n, part, d):
    nx = d // LANES
    pitch = _row_pitch(d)
    return jnp.concatenate([ref[pl.ds(part * nx + k, n, stride=pitch), :] for k in range(nx)], axis=1)


def _store_packed(ref, val, n, part, d):
    nx = d // LANES
    pitch = _row_pitch(d)
    for k in range(nx):
        ref[pl.ds(part * nx + k, n, stride=pitch), :] = val[:, k * LANES:(k + 1) * LANES]


def _load_stream(x_ref, packed):
    if not packed:
        return x_ref[...]
    n, d = packed
    return _load_packed(x_ref, n, 1, d)


def _stream_spec(tm, d, packed, index_map_row):
    if packed:
        return pl.BlockSpec((tm * _row_pitch(d), LANES), lambda *a: (index_map_row(*a), 0))
    return pl.BlockSpec((tm, d), lambda *a: (index_map_row(*a), 0))


def _mod_kernel(s_ref, w_ref, b_ref, o_ref):
    s = _silu(s_ref[...])
    o_ref[0] = _dot3(s, w_ref[0]) + b_ref[0]


def modulation_table(c, c_ctx, ada_w, ada_b):
    depth, d, n = ada_w.shape
    bsz = c.shape[0]
    s = jnp.zeros((SUBLANES, d), F32).at[:bsz].set(c).at[bsz].set(c_ctx)
    tn = 1024
    return pl.pallas_call(
        _mod_kernel,
        grid=(depth, n // tn),
        in_specs=[pl.BlockSpec((SUBLANES, d), lambda l, j: (0, 0)),
                  pl.BlockSpec((1, d, tn), lambda l, j: (l, 0, j)),
                  pl.BlockSpec((1, 1, tn), lambda l, j: (l, 0, j))],
        out_specs=pl.BlockSpec((1, SUBLANES, tn), lambda l, j: (l, 0, j)),
        out_shape=jax.ShapeDtypeStruct((depth, SUBLANES, n), F32),
        compiler_params=_cparams("parallel", "parallel"),
        name="mod_table",
    )(s, ada_w, ada_b.reshape(depth, 1, n))


def _proj_kernel(x_ref, g_ref, sh_ref, sc_ref, w_ref, o_ref, *, tn, packed):
    h = _rms_mod(_load_stream(x_ref, packed), g_ref[...], sh_ref[0], sc_ref[0]).astype(BF16)
    for j in range(w_ref.shape[1] // tn):
        o_ref[:, j * tn:(j + 1) * tn] = _dot(h, w_ref[:, j * tn:(j + 1) * tn])


def norm_mod_project(x, g, shift, scale, w, *, rows, rows_per_mod, tm, tn, packed):
    r = rows
    d, n = w.shape
    tm = min(tm, rows_per_mod)
    per = rows_per_mod // tm
    return pl.pallas_call(
        functools.partial(_proj_kernel, tn=tn, packed=(tm, d) if packed else None),
        grid=(r // tm,),
        in_specs=[_stream_spec(tm, d, packed, lambda i: i),
                  pl.BlockSpec((1, d), lambda i: (0, 0)),
                  pl.BlockSpec((1, 1, d), lambda i: (i // per, 0, 0)),
                  pl.BlockSpec((1, 1, d), lambda i: (i // per, 0, 0)),
                  pl.BlockSpec((d, n), lambda i: (0, 0))],
        out_specs=pl.BlockSpec((tm, n), lambda i: (i, 0)),
        out_shape=jax.ShapeDtypeStruct((r, n), F32),
        compiler_params=_cparams("parallel"),
        name="norm_mod_project",
    )(x, g.reshape(1, d), shift, scale, w)


def _tri(n, upper):
    r = lax.broadcasted_iota(I32, (n, n), 0)
    c = lax.broadcasted_iota(I32, (n, n), 1)
    return (r <= c) if upper else (r >= c)


SCAN_SUB = 256


def _block_tri(n, chunk, upper):
    r = lax.broadcasted_iota(I32, (n, n), 0)
    c = lax.broadcasted_iota(I32, (n, n), 1)
    same = (r // chunk) == (c // chunk)
    return same & ((r <= c) if upper else (r >= c))


def _hgrn_block(q, z, v, lb, st, upper):
    n = q.shape[0]
    ch = GLA_CHUNK
    nc = n // ch
    mask = _block_tri(n, ch, upper)
    qs = _silu(q)
    f = lb + (1.0 - lb) * _sigmoid(z)
    g = jnp.log(f)
    k = 1.0 - f
    c3 = _dot(mask.astype(BF16), jnp.concatenate(_split3(g), axis=1))
    dk = g.shape[1]
    cum = c3[:, 0:dk] + c3[:, dk:2 * dk] + c3[:, 2 * dk:3 * dk]
    cls = [cum[c * ch:c * ch + 1] if upper else cum[(c + 1) * ch - 1:(c + 1) * ch] for c in range(nc)]
    clb = jnp.concatenate([jnp.broadcast_to(cl, (ch, dk)) for cl in cls], axis=0)
    q_in = (qs * jnp.exp(cum)).astype(BF16)
    k_in = (k * jnp.exp(-cum)).astype(BF16)
    k_out = (k * jnp.exp(clb - cum)).astype(BF16)
    vb = v.astype(BF16)
    attn = jnp.where(mask, _dot_nt(q_in, k_in), 0.0)
    intra = _dot(attn.astype(BF16), vb)
    dv = v.shape[1]
    vt = v.T
    chunk_of_col = lax.broadcasted_iota(I32, (1, n), 1) // ch
    vexp_t = jnp.concatenate([jnp.where(chunk_of_col == c, vt, 0.0) for c in range(nc)], axis=0).astype(BF16)
    upd = _dot(vexp_t, k_out)
    sts = [None] * nc
    for c in (reversed(range(nc)) if upper else range(nc)):
        sts[c] = st
        st = st * jnp.exp(cls[c]) + upd[c * dv:(c + 1) * dv]
    chunk_of_row = lax.broadcasted_iota(I32, (n, 1), 0) // ch
    q_exp = jnp.concatenate([jnp.where(chunk_of_row == c, q_in, jnp.zeros_like(q_in)) for c in range(nc)], axis=1)
    st_cat = jnp.concatenate(sts, axis=1).astype(BF16)
    return intra + _dot_nt(q_exp, st_cat), st


def _hgrn_kernel(qf_ref, zf_ref, vf_ref, qb_ref, zb_ref, vb_ref,
                 qc_ref, zfc_ref, zbc_ref, vc_ref, lb_ref,
                 of_ref, ob_ref, ocf_ref, ocb_ref, stf_ref, stb_ref):
    lbf = lb_ref[0:1]
    lbb = lb_ref[1:2]

    def run(q_ref, z_ref, v_ref, lb, o_ref, st, upper):
        n = q_ref.shape[0]
        sub = min(SCAN_SUB, n)
        order = range(n // sub)
        for sb in (reversed(order) if upper else order):
            rs = slice(sb * sub, (sb + 1) * sub)
            o_ref[rs, :], st = _hgrn_block(q_ref[rs, :], z_ref[rs, :], v_ref[rs, :], lb, st, upper)
        return st

    @pl.when(pl.program_id(1) == 0)
    def _():
        z0 = jnp.zeros((HGRN_DK, HGRN_DK), F32)
        stf_ref[...] = run(qc_ref, zfc_ref, vc_ref, lbf, ocf_ref, z0, False)
        stb_ref[...] = run(qc_ref, zbc_ref, vc_ref, lbb, ocb_ref, z0, True)

    stf_ref[...] = run(qf_ref, zf_ref, vf_ref, lbf, of_ref, stf_ref[...], False)
    stb_ref[...] = run(qb_ref, zb_ref, vb_ref, lbb, ob_ref, stb_ref[...], True)


def hgrn_scan(p_lat, p_ctx, lb, *, bsz, seq, lc, tb):
    h = HGRN_HEADS
    dk = HGRN_DK
    nb = seq // tb
    cq, czf, czb, cv = 0, h, 2 * h, 3 * h

    def lat(col, rev):
        if rev:
            return pl.BlockSpec((tb, dk), lambda bh, i: ((bh // h) * nb + nb - 1 - i, col + bh % h))
        return pl.BlockSpec((tb, dk), lambda bh, i: ((bh // h) * nb + i, col + bh % h))

    def ctx(col):
        return pl.BlockSpec((lc, dk), lambda bh, i: (bh // h, col + bh % h))

    def out_lat(rev):
        if rev:
            return pl.BlockSpec((tb, dk), lambda bh, i: ((bh // h) * nb + nb - 1 - i, bh % h))
        return pl.BlockSpec((tb, dk), lambda bh, i: ((bh // h) * nb + i, bh % h))

    out_ctx = pl.BlockSpec((lc, dk), lambda bh, i: (bh // h, bh % h))
    w = h * dk
    return pl.pallas_call(
        _hgrn_kernel,
        grid=(bsz * h, nb),
        in_specs=[lat(cq, False), lat(czf, False), lat(cv, False),
                  lat(cq, True), lat(czb, True), lat(cv, True),
                  ctx(cq), ctx(czf), ctx(czb), ctx(cv),
                  pl.BlockSpec((2, dk), lambda bh, i: (0, bh % h))],
        out_specs=[out_lat(False), out_lat(True), out_ctx, out_ctx],
        out_shape=[jax.ShapeDtypeStruct((bsz * seq, w), F32), jax.ShapeDtypeStruct((bsz * seq, w), F32),
                   jax.ShapeDtypeStruct((bsz * lc, w), F32), jax.ShapeDtypeStruct((bsz * lc, w), F32)],
        scratch_shapes=[pltpu.VMEM((dk, dk), F32), pltpu.VMEM((dk, dk), F32)],
        compiler_params=_cparams("parallel", "arbitrary"),
        name="hgrn_scan",
    )(p_lat, p_lat, p_lat, p_lat, p_lat, p_lat, p_ctx, p_ctx, p_ctx, p_ctx, lb)


NA_RB = 8


def na_bias_table(rpb):
    w = GRID_W
    qc = np.arange(w)
    win0 = np.clip(qc - NA_KC // 2, 0, w - NA_KC)
    kc = np.arange(w)
    ok = (kc[None, :] >= win0[:, None]) & (kc[None, :] < win0[:, None] + NA_KC)
    dc = np.clip(kc[None, :] - qc[:, None] + NA_KC - 1, 0, 2 * NA_KC - 2)
    onehot = (np.arange(2 * NA_KC - 1)[:, None, None] == dc[None]).astype(np.float32)
    toep = jnp.einsum('hdm,mqc->hdqc', rpb.astype(F32), onehot, precision=lax.Precision.HIGHEST)
    toep = jnp.where(ok[None, None], toep, NEG_INF)
    t = jnp.stack([toep[:, NA_KR - 1 - v:2 * NA_KR - 1 - v] for v in range(NA_KR)], axis=1)
    t = jnp.transpose(t, (0, 1, 3, 2, 4))
    return t.reshape(rpb.shape[0], NA_KR, w, NA_KR * w)


def _na_kernel(q_ref, kp_ref, kc_ref, kn_ref, vp_ref, vc_ref, vn_ref, kx_ref, vx_ref, bias_ref,
               o_ref, ks_ref, vs_ref, *, rows):
    w = GRID_W
    blk = NA_RB * w
    i = pl.program_id(2)
    ks_ref[0:blk, :] = kp_ref[...].astype(BF16)
    ks_ref[blk:2 * blk, :] = kc_ref[...].astype(BF16)
    ks_ref[2 * blk:3 * blk, :] = kn_ref[...].astype(BF16)
    vs_ref[0:blk, :] = vp_ref[...].astype(BF16)
    vs_ref[blk:2 * blk, :] = vc_ref[...].astype(BF16)
    vs_ref[2 * blk:3 * blk, :] = vn_ref[...].astype(BF16)
    scale = NA_DH ** -0.5
    first_head = lax.broadcasted_iota(I32, (w, 2 * NA_DH), 1) < NA_DH
    qbd = []
    for rr in range(NA_RB):
        q2 = q_ref[rr * w:(rr + 1) * w, :] * scale
        qbd.append(jnp.concatenate([jnp.where(first_head, q2, 0.0), jnp.where(first_head, 0.0, q2)],
                                   axis=0).astype(BF16))
    sx = _dot_nt(jnp.concatenate(qbd, axis=0), kx_ref[...].astype(BF16))
    mx = jnp.max(sx, axis=-1, keepdims=True)
    px = jnp.exp(sx - mx)
    dx = jnp.sum(px, axis=-1, keepdims=True)
    ox = _dot(px.astype(BF16), vx_ref[...].astype(BF16))
    for rr in range(NA_RB):
        r = i * NA_RB + rr
        r0 = jnp.clip(r - NA_KR // 2, 0, rows - NA_KR)
        start = pl.multiple_of((r0 - (i - 1) * NA_RB) * w, w)
        vv = r - r0
        s = _dot_nt(qbd[rr], ks_ref[pl.ds(start, NA_KR * w), :])
        s = s + jnp.concatenate([bias_ref[0, vv], bias_ref[1, vv]], axis=0)
        mw = jnp.max(s, axis=-1, keepdims=True)
        p = jnp.exp(s - mw)
        dw = jnp.sum(p, axis=-1, keepdims=True)
        ow = _dot(p.astype(BF16), vs_ref[pl.ds(start, NA_KR * w), :])
        rs = slice(rr * 2 * w, (rr + 1) * 2 * w)
        m = jnp.maximum(mw, mx[rs])
        aw = jnp.exp(mw - m)
        ax = jnp.exp(mx[rs] - m)
        o = (ow * aw + ox[rs] * ax) / (dw * aw + dx[rs] * ax)
        o_ref[rr * w:(rr + 1) * w, :] = jnp.where(first_head, o[0:w], o[w:2 * w]).astype(o_ref.dtype)


def neighborhood_attention(p_lat, p_ctx, bias_tbl, *, bsz, seq, lc, col0):
    w = GRID_W
    rows = seq // w
    blk = NA_RB * w
    nb = rows // NA_RB
    hp = NA_HEADS // 2
    pw = 2 * NA_DH
    cq = col0 // pw
    ck = cq + hp
    cv = ck + hp

    def lat(col, off):
        def im(b, h, i):
            return (b * nb + jnp.clip(i + off, 0, nb - 1), col + h)
        return pl.BlockSpec((blk, pw), im)

    def ctx(col):
        return pl.BlockSpec((lc, pw), lambda b, h, i: (b, col + h))

    return pl.pallas_call(
        functools.partial(_na_kernel, rows=rows),
        grid=(bsz, hp, nb),
        in_specs=[lat(cq, 0), lat(ck, -1), lat(ck, 0), lat(ck, 1), lat(cv, -1), lat(cv, 0), lat(cv, 1),
                  ctx(ck), ctx(cv),
                  pl.BlockSpec((2, NA_KR, w, NA_KR * w), lambda b, h, i: (h, 0, 0, 0))],
        out_specs=pl.BlockSpec((blk, pw), lambda b, h, i: (b * nb + i, h)),
        out_shape=jax.ShapeDtypeStruct((bsz * seq, NA_WIDTH), BF16),
        scratch_shapes=[pltpu.VMEM((3 * blk, pw), BF16), pltpu.VMEM((3 * blk, pw), BF16)],
        compiler_params=_cparams("parallel", "parallel", "arbitrary"),
        name="neighborhood_attention",
    )(p_lat, p_lat, p_lat, p_lat, p_lat, p_lat, p_lat, p_ctx, p_ctx, bias_tbl)


def _ctx_attn_kernel(q_ref, k_ref, v_ref, o_ref):
    scale = NA_DH ** -0.5
    outs = []
    for hh in range(2):
        cs = slice(hh * NA_DH, (hh + 1) * NA_DH)
        q = (q_ref[:, cs] * scale).astype(BF16)
        s = _dot_nt(q, k_ref[:, cs].astype(BF16))
        p = jnp.exp(s - jnp.max(s, axis=-1, keepdims=True))
        o = _dot(p.astype(BF16), v_ref[:, cs].astype(BF16))
        outs.append(o / jnp.sum(p, axis=-1, keepdims=True))
    o_ref[...] = jnp.concatenate(outs, axis=-1).astype(o_ref.dtype)


def context_attention(p_ctx, *, bsz, lc, col0):
    hp = NA_HEADS // 2
    pw = 2 * NA_DH
    cq = col0 // pw

    def spec(col):
        return pl.BlockSpec((lc, pw), lambda b, h: (b, col + h))

    return pl.pallas_call(
        _ctx_attn_kernel,
        grid=(bsz, hp),
        in_specs=[spec(cq), spec(cq + hp), spec(cq + 2 * hp)],
        out_specs=pl.BlockSpec((lc, pw), lambda b, h: (b, h)),
        out_shape=jax.ShapeDtypeStruct((bsz * lc, NA_WIDTH), BF16),
        compiler_params=_cparams("parallel", "parallel"),
        name="context_attention",
    )(p_ctx, p_ctx, p_ctx)


def _residual_router(x, y, gate, g1, sh, sc, wrt, xa_ref, aff_ref):
    x1 = x + gate * y
    n, d = x1.shape
    _store_packed(xa_ref, x1, n, 0, d)
    _store_packed(xa_ref, x1, n, 1, d)
    pitch = _row_pitch(d)
    for k in range(2 * (d // LANES), pitch):
        xa_ref[pl.ds(k, n, stride=pitch), :] = jnp.zeros((n, LANES), F32)
    f = _rms_mod(x1, g1, sh, sc)
    lt = _dot3_nt(wrt, f)
    e = jnp.exp(lt - jnp.max(lt, axis=0, keepdims=True))
    aff_ref[0] = e / jnp.sum(e, axis=0, keepdims=True)


def _ab_out_kernel(of_ref, ob_ref, gg_ref, nl_ref, on_ref, w_ref, x_ref, gate_ref, g1_ref, sh_ref, sc_ref, wrt_ref,
                   xa_ref, aff_ref, *, packed):
    o = of_ref[...] + ob_ref[...]
    gain = on_ref[...]
    parts = []
    for h in range(HGRN_HEADS):
        oh = o[:, h * HGRN_DK:(h + 1) * HGRN_DK]
        parts.append(oh * lax.rsqrt(jnp.mean(oh * oh, axis=-1, keepdims=True) + RMS_EPS) * gain)
    hl = (jnp.concatenate(parts, axis=-1) * _silu(gg_ref[...])).astype(BF16)
    y = _dot(hl, w_ref[0:HGRN_WIDTH, :]) + _dot(nl_ref[...], w_ref[HGRN_WIDTH:, :])
    _residual_router(_load_stream(x_ref, packed), y, gate_ref[0], g1_ref[...], sh_ref[0], sc_ref[0], wrt_ref[...],
                     xa_ref, aff_ref)


def _ssd_out_kernel(yf_ref, yb_ref, xs_ref, z_ref, dsk_ref, ng_ref, w_ref, x_ref, gate_ref, g1_ref, sh_ref, sc_ref,
                    wrt_ref, xa_ref, aff_ref, *, group_width, packed):
    y = (yf_ref[...] + yb_ref[...] + dsk_ref[...] * xs_ref[...]) * _silu(z_ref[...])
    parts = []
    for g in range(y.shape[1] // group_width):
        yg = y[:, g * group_width:(g + 1) * group_width]
        parts.append(yg * lax.rsqrt(jnp.mean(yg * yg, axis=-1, keepdims=True) + RMS_EPS))
    yn = (jnp.concatenate(parts, axis=-1) * ng_ref[...]).astype(BF16)
    _residual_router(_load_stream(x_ref, packed), _dot(yn, w_ref[...]), gate_ref[0], g1_ref[...], sh_ref[0],
                     sc_ref[0], wrt_ref[...], xa_ref, aff_ref)


def _out_common(x, gate, g1, sh, sc, wrt, *, rows, rows_per_mod, tm, packed):
    r = rows
    e, d = wrt.shape
    per = rows_per_mod // tm
    nbatch = r // rows_per_mod
    pitch = _row_pitch(d)
    mod = lambda i: (i // per, 0, 0)
    in_specs = [_stream_spec(tm, d, packed, lambda i: i), pl.BlockSpec((1, 1, d), mod),
                pl.BlockSpec((1, d), lambda i: (0, 0)),
                pl.BlockSpec((1, 1, d), mod), pl.BlockSpec((1, 1, d), mod), pl.BlockSpec((e, d), lambda i: (0, 0))]
    out_specs = [pl.BlockSpec((tm * pitch, LANES), lambda i: (i, 0)),
                 pl.BlockSpec((1, e, tm), lambda i: (i // per, 0, i % per))]
    out_shape = [jax.ShapeDtypeStruct((r * pitch, LANES), F32), jax.ShapeDtypeStruct((nbatch, e, rows_per_mod), F32)]
    args = (x, gate, g1.reshape(1, d), sh, sc, wrt)
    return in_specs, out_specs, out_shape, args


def ab_out(o_f, o_b, p, nl, onorm_g, w_out, x, gate, g1, sh, sc, wrt, *, rows_per_mod, tm, packed):
    tm = min(tm, rows_per_mod)
    r = o_f.shape[0]
    wd = HGRN_WIDTH
    row = lambda i: (i, 0)
    d = wrt.shape[1]
    c_in, c_out, c_shape, c_args = _out_common(x, gate, g1, sh, sc, wrt, rows=r, rows_per_mod=rows_per_mod, tm=tm,
                                               packed=packed)
    in_specs = [pl.BlockSpec((tm, wd), row), pl.BlockSpec((tm, wd), row),
                pl.BlockSpec((tm, wd), lambda i: (i, 4)),
                pl.BlockSpec((tm, NA_WIDTH), row),
                pl.BlockSpec((1, HGRN_DK), lambda i: (0, 0)),
                pl.BlockSpec(w_out.shape, lambda i: (0, 0))] + c_in
    return pl.pallas_call(
        functools.partial(_ab_out_kernel, packed=(tm, d) if packed else None),
        grid=(r // tm,), in_specs=in_specs, out_specs=c_out, out_shape=c_shape,
        compiler_params=_cparams("parallel"), name="ab_out",
    )(o_f, o_b, p, nl, onorm_g.reshape(1, HGRN_DK), w_out, *c_args)


def ssd_out(y_f, y_b, xa, p, dsk, ng, w_out, x, gate, g1, sh, sc, wrt, *, rows_per_mod, tm, packed):
    tm = min(tm, rows_per_mod)
    r = y_f.shape[0]
    inner = w_out.shape[0]
    row = lambda i: (i, 0)
    d = wrt.shape[1]
    c_in, c_out, c_shape, c_args = _out_common(x, gate, g1, sh, sc, wrt, rows=r, rows_per_mod=rows_per_mod, tm=tm,
                                               packed=packed)
    in_specs = [pl.BlockSpec((tm, inner), row), pl.BlockSpec((tm, inner), row),
                pl.BlockSpec((tm, inner), row),
                pl.BlockSpec((tm, inner), row),
                pl.BlockSpec((1, inner), lambda i: (0, 0)), pl.BlockSpec((1, inner), lambda i: (0, 0)),
                pl.BlockSpec(w_out.shape, lambda i: (0, 0))] + c_in
    return pl.pallas_call(
        functools.partial(_ssd_out_kernel, group_width=inner // SSD_GROUPS, packed=(tm, d) if packed else None),
        grid=(r // tm,), in_specs=in_specs, out_specs=c_out, out_shape=c_shape,
        compiler_params=_cparams("parallel"), name="ssd_out",
    )(y_f, y_b, xa, p, dsk.reshape(1, inner), ng.reshape(1, inner), w_out, *c_args)


F32_INF_BITS = 0x7F800000


def _lane_cumsum(src_ref, dst_ref, n_chunks):
    e = src_ref.shape[0]
    upper = _tri(LANES, True).astype(BF16)

    def body(j, off):
        ds = pl.ds(pl.multiple_of(j * LANES, LANES), LANES)
        c = _dot(src_ref[:, ds].astype(BF16), upper) + off
        dst_ref[:, ds] = c
        return c[:, LANES - 1:LANES]

    lax.fori_loop(0, n_chunks, body, jnp.zeros((e, 1), F32), unroll=math.gcd(n_chunks, 8))


def _select_kernel(aff_ref, idx_ref, gate_ref, m_scr, c_scr, *, seq, cap, nph, kc, row_scale):
    e = aff_ref.shape[1]
    n_chunks = seq // LANES
    bits = pltpu.bitcast(aff_ref[0], I32)

    def bisect(_, lohi):
        lo, hi = lohi
        mid = lo + lax.shift_right_logical(hi - lo, 1)
        cnt = jnp.sum((bits >= mid).astype(F32), axis=1, keepdims=True)
        ge = cnt >= cap
        return jnp.where(ge, mid, lo), jnp.where(ge, hi, mid)

    thr, _ = lax.fori_loop(0, 31, bisect, (jnp.zeros((e, 1), I32), jnp.full((e, 1), F32_INF_BITS, I32)))
    gt = bits > thr
    eq = bits == thr
    need = cap - jnp.sum(gt.astype(F32), axis=1, keepdims=True)
    m_scr[...] = eq.astype(F32)
    _lane_cumsum(m_scr, c_scr, n_chunks)
    sel = gt | (eq & (c_scr[...] <= need))
    m_scr[...] = sel.astype(F32)
    _lane_cumsum(m_scr, c_scr, n_chunks)

    sub_p = lax.broadcasted_iota(I32, (nph, kc), 0).astype(F32)
    sub_l = lax.broadcasted_iota(I32, (LANES, kc), 0).astype(F32)
    lane_t = lax.broadcasted_iota(I32, (1, kc), 1)
    inv = 1.0 / LANES
    for ex in range(e):
        def body(c, acc):
            ds = pl.ds(pl.multiple_of(c * kc, LANES), kc)
            pos = c_scr[ex:ex + 1, ds] - 1.0
            phi = jnp.floor(pos * inv)
            plo = pos - LANES * phi
            hit = (phi == sub_p) & (m_scr[ex:ex + 1, ds] > 0.0)
            t = (lane_t + c * kc).astype(F32)
            th = jnp.floor(t * inv)
            tl = t - LANES * th
            a1, a2, a3 = _split3(aff_ref[0, ex:ex + 1, ds])
            rows = [th, tl, a1.astype(F32), a2.astype(F32), a3.astype(F32)]
            lhs = jnp.concatenate([jnp.where(hit, r, 0.0) for r in rows], axis=0).astype(BF16)
            onehot = (plo == sub_l).astype(BF16)
            return acc + _dot_nt(lhs, onehot)

        acc = lax.fori_loop(0, seq // kc, body, jnp.zeros((5 * nph, LANES), F32), unroll=math.gcd(seq // kc, 2))
        token = (acc[0:nph] * LANES + acc[nph:2 * nph]).astype(I32)
        idx_ref[0, ex] = (pl.program_id(0) * seq + token) * row_scale
        gate_ref[0, ex] = acc[2 * nph:3 * nph] + acc[3 * nph:4 * nph] + acc[4 * nph:5 * nph]


def expert_select(aff_t, cap, row_scale):
    bsz, e, seq = aff_t.shape
    nph = -(-cap // LANES)
    nph = -(-nph // SUBLANES) * SUBLANES
    kc = min(2048, seq)
    idx, gate = pl.pallas_call(
        functools.partial(_select_kernel, seq=seq, cap=cap, nph=nph, kc=kc, row_scale=row_scale),
        grid=(bsz,),
        in_specs=[pl.BlockSpec((1, e, seq), lambda b: (b, 0, 0))],
        out_specs=[pl.BlockSpec((1, e, nph, LANES), lambda b: (b, 0, 0, 0)),
                   pl.BlockSpec((1, e, nph, LANES), lambda b: (b, 0, 0, 0))],
        out_shape=[jax.ShapeDtypeStruct((bsz, e, nph, LANES), I32),
                   jax.ShapeDtypeStruct((bsz, e, nph, LANES), F32)],
        scratch_shapes=[pltpu.VMEM((e, seq), F32), pltpu.VMEM((e, seq), F32)],
        compiler_params=_cparams("parallel"),
        name="expert_select",
    )(aff_t)
    return idx.reshape(bsz, e, nph * LANES)[:, :, :cap], gate.reshape(bsz, e, nph * LANES)[:, :, :cap]


FFN_SLOTS = 3
FFN_TILE = 512


def _ffn_kernel(idx_ref, gate_ref, g1_ref, sh_ref, sc_ref, g5_ref, w1_ref, w3_ref, w2_ref, xa_in,
                xa_hbm, buf, gsem, ssem, *, tr, nt, d):
    del xa_in
    nx = d // LANES
    pitch = _row_pitch(d)
    grp = min(8, tr)
    j = pl.program_id(2)
    step = (pl.program_id(0) * pl.num_programs(1) + pl.program_id(1)) * nt + j
    slot = step % FFN_SLOTS
    nslot = (step + 1) % FFN_SLOTS

    def start_gathers(tile, sl):
        def body(i, c):
            for k in range(grp):
                src = idx_ref[0, 0, tile * tr + i * grp + k]
                dst = i * (grp * pitch) + k * pitch
                pltpu.make_async_copy(xa_hbm.at[pl.ds(src, 2 * nx)], buf.at[sl, pl.ds(dst, 2 * nx)],
                                      gsem.at[sl]).start(priority=k % 2)
            return c
        lax.fori_loop(0, tr // grp, body, 0)

    def start_scatters(tile, sl):
        def body(i, c):
            for k in range(grp):
                dst = idx_ref[0, 0, tile * tr + i * grp + k] + nx
                src = i * (grp * pitch) + k * pitch + nx
                pltpu.make_async_copy(buf.at[sl, pl.ds(src, nx)], xa_hbm.at[pl.ds(dst, nx)],
                                      ssem.at[sl]).start(priority=k % 2)
            return c
        lax.fori_loop(0, tr // grp, body, 0)

    def wait_gathers(sl):
        n = tr * 2 * nx
        pltpu.make_async_copy(xa_hbm.at[pl.ds(0, n)], buf.at[sl, pl.ds(0, n)], gsem.at[sl]).wait()

    def wait_scatters(sl):
        n = tr * nx
        pltpu.make_async_copy(buf.at[sl, pl.ds(0, n)], xa_hbm.at[pl.ds(0, n)], ssem.at[sl]).wait()

    @pl.when(j == 0)
    def _():
        start_gathers(0, slot)

    @pl.when(j + 1 < nt)
    def _():
        @pl.when(j >= FFN_SLOTS - 1)
        def _():
            wait_scatters(nslot)

        start_gathers(j + 1, nslot)

    wait_gathers(slot)
    rows = buf.at[slot]
    f = _rms_mod(_load_packed(rows, tr, 0, d), g1_ref[...], sh_ref[0], sc_ref[0]).astype(BF16)
    hid = (_silu(_dot(f, w1_ref[0, 0])) * _dot(f, w3_ref[0, 0])).astype(BF16)
    y = _dot(hid, w2_ref[0, 0])
    eye = lax.broadcasted_iota(I32, (tr, tr), 0) == lax.broadcasted_iota(I32, (tr, tr), 1)
    gcol = jnp.sum(jnp.where(eye, gate_ref[0], 0.0), axis=1, keepdims=True)
    _store_packed(rows, _load_packed(rows, tr, 1, d) + (g5_ref[0] * gcol) * y, tr, 1, d)
    start_scatters(j, slot)

    @pl.when(j == nt - 1)
    def _():
        for back in range(min(FFN_SLOTS, nt)):
            wait_scatters((step - back) % FFN_SLOTS)


def expert_ffn_combine(xa, idx, gate, g1, sh, sc, g5, w1, w3, w2, *, layer):
    bsz, e, cap = idx.shape
    d, ff = w1.shape[2], w1.shape[3]
    tr = min(FFN_TILE, cap)
    nt = cap // tr
    mod = lambda b, ex, j: (b, 0, 0)
    return pl.pallas_call(
        functools.partial(_ffn_kernel, tr=tr, nt=nt, d=d),
        grid=(bsz, e, nt),
        in_specs=[pl.BlockSpec((1, 1, cap), lambda b, ex, j: (b * e + ex, 0, 0), memory_space=pltpu.SMEM),
                  pl.BlockSpec((1, 1, tr), lambda b, ex, j: ((b * e + ex) * nt + j, 0, 0)),
                  pl.BlockSpec((1, d), lambda b, ex, j: (0, 0)),
                  pl.BlockSpec((1, 1, d), mod), pl.BlockSpec((1, 1, d), mod), pl.BlockSpec((1, 1, d), mod),
                  pl.BlockSpec((1, 1, d, ff), lambda b, ex, j: (layer, ex, 0, 0)),
                  pl.BlockSpec((1, 1, d, ff), lambda b, ex, j: (layer, ex, 0, 0)),
                  pl.BlockSpec((1, 1, ff, d), lambda b, ex, j: (layer, ex, 0, 0)),
                  pl.BlockSpec(memory_space=pl.ANY)],
        out_specs=pl.BlockSpec(memory_space=pl.ANY),
        out_shape=jax.ShapeDtypeStruct(xa.shape, F32),
        scratch_shapes=[pltpu.VMEM((FFN_SLOTS, tr * _row_pitch(d), LANES), F32),
                        pltpu.SemaphoreType.DMA((FFN_SLOTS,)), pltpu.SemaphoreType.DMA((FFN_SLOTS,))],
        input_output_aliases={9: 0},
        compiler_params=_cparams("arbitrary", "arbitrary", "arbitrary"),
        name="expert_ffn_combine",
    )(idx.reshape(bsz * e, 1, cap), gate.reshape(bsz * e * nt, 1, tr), g1.reshape(1, d), sh, sc, g5,
      w1, w3, w2, xa)


CONV_HALO = 8


def _ssd_in_kernel(cur_ref, prev_ref, next_ref, g_ref, sh_ref, sc_ref, w_ref, cw_ref, cb_ref, dtb_ref,
                   z_ref, xa_ref, dtv_ref, ext_ref, *, tm, per, inner, cdim, packed):
    i = pl.program_id(0)
    first = (i % per) == 0
    last = (i % per) == per - 1
    halo = (CONV_HALO, packed[1]) if packed else None

    def normed(ref, pk):
        return _rms_mod(_load_stream(ref, pk), g_ref[...], sh_ref[0], sc_ref[0]).astype(BF16)

    h = normed(cur_ref, packed)
    hp = normed(prev_ref, halo)
    hn = normed(next_ref, halo)
    z_ref[...] = _dot(h, w_ref[:, 0:inner])
    v = _dot(h, w_ref[:, inner + cdim:inner + cdim + LANES]) + dtb_ref[...]
    dtv_ref[...] = jnp.maximum(v, 0.0) + jnp.log1p(jnp.exp(-jnp.abs(v)))
    half = cdim // 2
    lo = CONV_HALO - SSD_CONV // 2
    for j in range(2):
        cols = slice(inner + j * half, inner + (j + 1) * half)
        out = slice(j * half, (j + 1) * half)
        ext_ref[0:CONV_HALO, :] = jnp.where(first, 0.0, _dot(hp, w_ref[:, cols]))
        ext_ref[CONV_HALO:CONV_HALO + tm, :] = _dot(h, w_ref[:, cols])
        ext_ref[CONV_HALO + tm:2 * CONV_HALO + tm, :] = jnp.where(last, 0.0, _dot(hn, w_ref[:, cols]))
        acc = cb_ref[:, out] + cw_ref[0:1, out] * ext_ref[pl.ds(lo, tm), :]
        for k in range(1, SSD_CONV):
            acc = acc + cw_ref[k:k + 1, out] * ext_ref[pl.ds(lo + k, tm), :]
        xa_ref[:, out] = _silu(acc)


def ssd_in(x, g, shift, scale, w, conv_w, conv_b, dt_bias, *, rows, rows_per_seq, tm, inner, packed):
    r = rows
    d = w.shape[0]
    cdim = conv_w.shape[1]
    tm = min(tm, rows_per_seq)
    per = rows_per_seq // tm
    hb = tm // CONV_HALO
    nhb = r // CONV_HALO
    const = lambda i: (0, 0)
    mod = lambda i: (i // per, 0, 0)
    return pl.pallas_call(
        functools.partial(_ssd_in_kernel, tm=tm, per=per, inner=inner, cdim=cdim,
                          packed=(tm, d) if packed else None),
        grid=(r // tm,),
        in_specs=[_stream_spec(tm, d, packed, lambda i: i),
                  _stream_spec(CONV_HALO, d, packed, lambda i: jnp.maximum(i * hb - 1, 0)),
                  _stream_spec(CONV_HALO, d, packed, lambda i: jnp.minimum((i + 1) * hb, nhb - 1)),
                  pl.BlockSpec((1, d), const), pl.BlockSpec((1, 1, d), mod), pl.BlockSpec((1, 1, d), mod),
                  pl.BlockSpec(w.shape, const), pl.BlockSpec((SSD_CONV, cdim), const),
                  pl.BlockSpec((1, cdim), const), pl.BlockSpec((1, LANES), const)],
        out_specs=[pl.BlockSpec((tm, inner), lambda i: (i, 0)), pl.BlockSpec((tm, cdim), lambda i: (i, 0)),
                   pl.BlockSpec((tm, LANES), lambda i: (i, 0))],
        out_shape=[jax.ShapeDtypeStruct((r, inner), F32), jax.ShapeDtypeStruct((r, cdim), F32),
                   jax.ShapeDtypeStruct((r, LANES), F32)],
        scratch_shapes=[pltpu.VMEM((tm + 2 * CONV_HALO, cdim // 2), F32)],
        compiler_params=_cparams("parallel"),
        name="ssd_in",
    )(x, x, x, g.reshape(1, d), shift, scale, w, conv_w, conv_b.reshape(1, cdim), dt_bias)


SSD_HPG = 4


def _ssd_block(x, bm, cm, dtv, a_row, hs, lane0, upper, need_y=True):
    t = x.shape[0]
    hp = x.shape[1]
    pdim = hp // SSD_HPG
    ch = SSD_CHUNK
    nc = t // ch
    lane = lax.broadcasted_iota(I32, (1, LANES), 1)
    head_of_lane = lax.broadcasted_iota(I32, (1, hp), 1) // pdim
    dtx = jnp.zeros((t, hp), F32)
    a_x = jnp.zeros((1, hp), F32)
    for h in range(SSD_HPG):
        pick = lane == lane0 + h
        dtx = jnp.where(head_of_lane == h, jnp.sum(jnp.where(pick, dtv, 0.0), axis=1, keepdims=True), dtx)
        a_x = jnp.where(head_of_lane == h, jnp.sum(jnp.where(pick, a_row, 0.0), axis=1, keepdims=True), a_x)
    dtax = dtx * a_x
    c3 = _dot(_block_tri(t, ch, upper).astype(BF16), jnp.concatenate(_split3(dtax), axis=1))
    acx = c3[:, 0:hp] + c3[:, hp:2 * hp] + c3[:, 2 * hp:3 * hp]
    r = lax.broadcasted_iota(I32, (ch, hp), 0)
    s = lax.broadcasted_iota(I32, (ch, hp), 1) % ch
    src_le = (s <= r) if upper else (s >= r)
    ident = s == r
    causal = (r <= s) if upper else (r >= s)
    same_head = (lax.broadcasted_iota(I32, (hp, hp), 0) // pdim) == (lax.broadcasted_iota(I32, (hp, hp), 1) // pdim)
    ydiag, decay_out, alasts, upds = [], [], [], []
    for c in range(nc):
        rs = slice(c * ch, (c + 1) * ch)
        acx_c, dtx_c, x_c = acx[rs], dtx[rs], x[rs]
        bm_c = bm[rs].astype(BF16)
        alast = acx_c[0:1] if upper else acx_c[ch - 1:ch]
        if need_y:
            arx = jnp.sum(jnp.where(src_le, dtax[rs], 0.0), axis=0, keepdims=True)
            dtr = jnp.sum(jnp.where(ident, dtx_c, 0.0), axis=0, keepdims=True)
            dec = jnp.where(causal, jnp.exp(jnp.minimum(acx_c - arx, 0.0)), 0.0)
            cbx = _dot_nt(cm[rs].astype(BF16), jnp.concatenate([bm_c] * SSD_HPG, axis=0))
            gx = (cbx * dec * dtr).astype(BF16)
            xbd = jnp.where(same_head, jnp.concatenate([x_c] * SSD_HPG, axis=0), 0.0).astype(BF16)
            ydiag.append(_dot(gx, xbd))
            decay_out.append(jnp.exp(acx_c))
        alasts.append(alast)
        upds.append(_dot_tn(bm_c, (x_c * (jnp.exp(alast - acx_c) * dtx_c)).astype(BF16)))
    ys = [None] * nc
    for c in (reversed(range(nc)) if upper else range(nc)):
        if need_y:
            ys[c] = ydiag[c] + _dot(cm[c * ch:(c + 1) * ch].astype(BF16), hs.astype(BF16)) * decay_out[c]
        hs = hs * jnp.exp(alasts[c]) + upds[c]
    return (jnp.concatenate(ys, axis=0) if need_y else None), hs


def _ssd_kernel(xf_ref, bf_ref, cf_ref, df_ref, xb_ref, bb_ref, cb_ref, db_ref,
                xc_ref, bc_ref, cc_ref, dc_ref, alog_ref, yf_ref, yb_ref, hf_ref, hb_ref):
    g = pl.program_id(0) % SSD_GROUPS
    a_row = -jnp.exp(alog_ref[...])
    lane_f = g * SSD_HPG
    lane_b = SSD_GROUPS * SSD_HPG + g * SSD_HPG

    def run(x_ref, b_ref, c_ref, d_ref, y_ref, hs, lane0, upper):
        n = x_ref.shape[0]
        sub = min(SCAN_SUB, n)
        order = range(n // sub)
        for sb in (reversed(order) if upper else order):
            rs = slice(sb * sub, (sb + 1) * sub)
            y, hs = _ssd_block(x_ref[rs, :], b_ref[rs, :], c_ref[rs, :], d_ref[rs, :], a_row, hs, lane0, upper,
                               need_y=y_ref is not None)
            if y_ref is not None:
                y_ref[rs, :] = y
        return hs

    @pl.when(pl.program_id(1) == 0)
    def _():
        z0 = jnp.zeros(hf_ref.shape, F32)
        hf_ref[...] = run(xc_ref, bc_ref, cc_ref, dc_ref, None, z0, lane_f, False)
        hb_ref[...] = run(xc_ref, bc_ref, cc_ref, dc_ref, None, z0, lane_b, True)

    hf_ref[...] = run(xf_ref, bf_ref, cf_ref, df_ref, yf_ref, hf_ref[...], lane_f, False)
    hb_ref[...] = run(xb_ref, bb_ref, cb_ref, db_ref, yb_ref, hb_ref[...], lane_b, True)


def ssd_scan(xa, dtv, xa_c, dtv_c, a_log_row, *, bsz, seq, lc, tb, inner):
    g = SSD_GROUPS
    n = SSD_STATE
    xw = inner // g
    nb = seq // tb
    cb0 = inner // n
    cc0 = cb0 + g

    def blk(i, rev):
        return nb - 1 - i if rev else i

    def lat(width, col, rev):
        return pl.BlockSpec((tb, width), lambda bg, i: ((bg // g) * nb + blk(i, rev), col + bg % g))

    def lat_dt(rev):
        return pl.BlockSpec((tb, LANES), lambda bg, i: ((bg // g) * nb + blk(i, rev), 0))

    def ctx(width, col):
        return pl.BlockSpec((lc, width), lambda bg, i: (bg // g, col + bg % g))

    def out(rev):
        return pl.BlockSpec((tb, xw), lambda bg, i: ((bg // g) * nb + blk(i, rev), bg % g))

    return pl.pallas_call(
        _ssd_kernel,
        grid=(bsz * g, nb),
        in_specs=[lat(xw, 0, False), lat(n, cb0, False), lat(n, cc0, False), lat_dt(False),
                  lat(xw, 0, True), lat(n, cb0, True), lat(n, cc0, True), lat_dt(True),
                  ctx(xw, 0), ctx(n, cb0), ctx(n, cc0), pl.BlockSpec((lc, LANES), lambda bg, i: (bg // g, 0)),
                  pl.BlockSpec((1, LANES), lambda bg, i: (0, 0))],
        out_specs=[out(False), out(True)],
        out_shape=[jax.ShapeDtypeStruct((bsz * seq, inner), F32), jax.ShapeDtypeStruct((bsz * seq, inner), F32)],
        scratch_shapes=[pltpu.VMEM((n, xw), F32), pltpu.VMEM((n, xw), F32)],
        compiler_params=_cparams("parallel", "arbitrary"),
        name="ssd_scan",
    )(xa, xa, xa, dtv, xa, xa, xa, dtv, xa_c, xa_c, xa_c, dtv_c, a_log_row)


def _final_norm_kernel(x_ref, g_ref, o_ref, *, packed):
    x = _load_stream(x_ref, packed)
    o_ref[...] = x * lax.rsqrt(jnp.mean(x * x, axis=-1, keepdims=True) + RMS_EPS) * g_ref[...]


def final_norm(x, g, *, rows, tm, packed):
    r = rows
    d = g.shape[0]
    return pl.pallas_call(
        functools.partial(_final_norm_kernel, packed=(tm, d) if packed else None),
        grid=(r // tm,),
        in_specs=[_stream_spec(tm, d, packed, lambda i: i), pl.BlockSpec((1, d), lambda i: (0, 0))],
        out_specs=pl.BlockSpec((tm, d), lambda i: (i, 0)),
        out_shape=jax.ShapeDtypeStruct((r, d), F32),
        compiler_params=_cparams("parallel"),
        name="final_norm",
    )(x, g.reshape(1, d))


def _moe(xa, aff, g1, sh, sc, g5, w1, w3, w2, *, seq, layer):
    cap = max(1, CAPACITY_FACTOR * seq // N_EXPERTS)
    idx, gate = expert_select(aff, cap, _row_pitch(w1.shape[2]))
    return expert_ffn_combine(xa, idx, gate, g1, sh, sc, g5, w1, w3, w2, layer=layer)


def kernel(x, c, ctx, c_ctx, ada_w, ada_b, norm_g, final_g, ab_w_in, ab_w_out, hgrn_lb_logits, hgrn_onorm_g, na_rpb, ssd_w_in, ssd_conv_w, ssd_conv_b, ssd_a_log, ssd_dt_bias, ssd_d, ssd_norm_g, ssd_w_out, moe_router, moe_w1, moe_w3, moe_w2):
    bsz, seq, d = x.shape
    lc = ctx.shape[1]
    depth = ada_w.shape[0]
    assert depth == 2, "context outputs of an SSD layer are not implemented (only needed when a layer follows it)"
    mods = modulation_table(c, c_ctx, ada_w, ada_b)
    lb_all = jnp.cumsum(jax.nn.softmax(hgrn_lb_logits.astype(F32), axis=1), axis=1)
    xl = x.reshape(bsz * seq, d)
    xc = ctx.reshape(bsz * lc, d)
    packed = False
    w1, w3, w2 = (w.astype(BF16) for w in (moe_w1, moe_w3, moe_w2))
    hg = 5 * HGRN_WIDTH
    for l in range(depth):
        need_ctx = l < depth - 1
        k = l // 2
        m = mods[l]
        ml = [m[:bsz, j * d:(j + 1) * d].reshape(bsz, 1, d) for j in range(6)]
        mc = [jnp.broadcast_to(m[bsz, j * d:(j + 1) * d].reshape(1, 1, d), (bsz, 1, d)) for j in range(6)]
        wrt = moe_router[l].T
        g0, g1 = norm_g[l, 0], norm_g[l, 1]
        if l % 2 == 0:
            w_in = ab_w_in[k].astype(BF16)
            w_out = ab_w_out[k].astype(BF16)
            p_lat = norm_mod_project(xl, g0, ml[0], ml[1], w_in, rows=bsz * seq, rows_per_mod=seq, tm=512, tn=1024,
                                     packed=packed)
            p_ctx = norm_mod_project(xc, g0, mc[0], mc[1], w_in, rows=bsz * lc, rows_per_mod=lc, tm=512, tn=1024,
                                     packed=packed)
            o_f, o_b, oc_f, oc_b = hgrn_scan(p_lat, p_ctx, lb_all[:, k], bsz=bsz, seq=seq, lc=lc, tb=min(1024, seq))
            nl = neighborhood_attention(p_lat, p_ctx, na_bias_table(na_rpb[k]), bsz=bsz, seq=seq, lc=lc, col0=hg)
            xa, aff = ab_out(o_f, o_b, p_lat, nl, hgrn_onorm_g[k], w_out, xl, ml[2], g1, ml[3], ml[4], wrt,
                             rows_per_mod=seq, tm=512, packed=packed)
            if need_ctx:
                nc = context_attention(p_ctx, bsz=bsz, lc=lc, col0=hg)
                ca, caff = ab_out(oc_f, oc_b, p_ctx, nc, hgrn_onorm_g[k], w_out, xc, mc[2], g1, mc[3], mc[4],
                                  wrt, rows_per_mod=lc, tm=512, packed=packed)
        else:
            inner = ssd_w_out.shape[1]
            heads2 = ssd_dt_bias.shape[1] * ssd_dt_bias.shape[2]
            wz = ssd_w_in[k]
            pad = jnp.zeros((d, LANES - heads2), F32)
            w_in = jnp.concatenate([wz, pad], axis=1).astype(BF16)
            w_out = ssd_w_out[k].astype(BF16)
            dtb = jnp.zeros((1, LANES), F32).at[0, :heads2].set(ssd_dt_bias[k].reshape(-1))
            alog = jnp.zeros((1, LANES), F32).at[0, :heads2].set(ssd_a_log[k].reshape(-1))
            proj = functools.partial(ssd_in, g=g0, w=w_in, conv_w=ssd_conv_w[k], conv_b=ssd_conv_b[k], dt_bias=dtb,
                                     tm=256, inner=inner, packed=packed)
            z, xs, dtv = proj(xl, shift=ml[0], scale=ml[1], rows=bsz * seq, rows_per_seq=seq)
            _, xs_c, dtv_c = proj(xc, shift=mc[0], scale=mc[1], rows=bsz * lc, rows_per_seq=lc)
            y_f, y_b = ssd_scan(xs, dtv, xs_c, dtv_c, alog, bsz=bsz, seq=seq, lc=lc, tb=min(1024, seq), inner=inner)
            dsk = jnp.repeat(ssd_d[k].astype(F32), SSD_HEADDIM)
            xa, aff = ssd_out(y_f, y_b, xs, z, dsk, ssd_norm_g[k], w_out, xl, ml[2], g1, ml[3], ml[4], wrt,
                              rows_per_mod=seq, tm=256, packed=packed)
        xl = _moe(xa, aff, g1, ml[3], ml[4], ml[5], w1, w3, w2, seq=seq, layer=l)
        if need_ctx:
            xc = _moe(ca, caff, g1, mc[3], mc[4], mc[5], w1, w3, w2, seq=lc, layer=l)
        packed = True
    return final_norm(xl, final_g, rows=bsz * seq, tm=512, packed=packed).reshape(bsz, seq, d)
```
